```python
import jax, jax.numpy as jnp
from jax import lax
import numpy as np

D_MODEL = 1024
BATCH = 8
SEQ = 4096
DEPTH = 2
DEC_BATCH = 4
DEC_SEQ = 4096
PAST_LEN = 128

GRID_W = 64
HEAD_DIM = 64
D_RWKV = 512
N_RWKV_HEADS = D_RWKV // HEAD_DIM
D_NAT = 512
N_NAT_HEADS = D_NAT // HEAD_DIM
LORA_W = 64
LORA_A = 64
LORA_G = 128
N_DIR = 2
WIN_H = 8
WIN_W = 16
N_MEM = 256
N_XATTN_HEADS = 4
XATTN_HEAD_DIM = D_MODEL // N_XATTN_HEADS
D_FF = 4 * D_MODEL
NORM_EPS = 1e-6
GN_EPS = 1e-5 * HEAD_DIM

RWKV_SPLITS = [D_RWKV, 2 * D_RWKV, 3 * D_RWKV,
               3 * D_RWKV + N_DIR * LORA_W,
               3 * D_RWKV + N_DIR * LORA_W + N_DIR * LORA_A]
RWKV_COLS = 3 * D_RWKV + N_DIR * LORA_W + N_DIR * LORA_A + LORA_G
NAT_COLS = 3 * D_NAT
GATE_COLS = 2 * D_MODEL
D_IN = RWKV_COLS + NAT_COLS + GATE_COLS

kernel_name = "rwkv7_natten_hybrid_encoder"


def rmsnorm(x, g):
    xf = x.astype(jnp.float32)
    y = xf * lax.rsqrt(jnp.mean(xf * xf, axis=-1, keepdims=True) + NORM_EPS)
    return (y * g.astype(jnp.float32)).astype(x.dtype)


def centred_shift(p, mu_prev, mu_next):
    zero = jnp.zeros_like(p[:, :1])
    p_prev = jnp.concatenate([zero, p[:, :-1]], axis=1)
    p_next = jnp.concatenate([p[:, 1:], zero], axis=1)
    return p + mu_prev * (p_prev - p) + mu_next * (p_next - p)


def _wkv7_step(S, inp):
    r, w, k, v, a, b = inp
    Sa = jnp.einsum('bhij,bhj->bhi', S, a)
    S = S * w[:, :, None, :] + Sa[..., None] * b[:, :, None, :] + v[..., None] * k[:, :, None, :]
    return S, jnp.einsum('bhij,bhj->bhi', S, r)


def wkv7_scan(r, w, k, v, a, b, reverse):
    B, T, H, N = r.shape
    xs = tuple(jnp.moveaxis(t, 1, 0) for t in (r, w, k, v, a, b))
    S0 = jnp.zeros((B, H, N, N), jnp.float32)
    _, y = lax.scan(_wkv7_step, S0, xs, reverse=reverse)
    return jnp.moveaxis(y, 0, 1)


def rwkv7_branch(p, mu_prev, mu_next, w0, w_up, a0, a_up, g_up, k_k, k_a, r_k, gn_g, gn_b):
    B, T, _ = p.shape
    H, N = N_RWKV_HEADS, HEAD_DIM
    f32 = jnp.float32
    p = centred_shift(p.astype(f32), mu_prev.astype(f32), mu_next.astype(f32))
    r, k, v, wd, ad, gd = jnp.split(p, RWKV_SPLITS, axis=-1)
    wd = wd.reshape(B, T, N_DIR, LORA_W)
    ad = ad.reshape(B, T, N_DIR, LORA_A)
    w_raw = w0.astype(f32) + jnp.einsum('btdl,dlc->btdc', jnp.tanh(wd), w_up.astype(f32))
    decay = jnp.exp(-jnp.exp(-jax.nn.softplus(-w_raw) - 0.5))
    a = jax.nn.sigmoid(a0.astype(f32) + jnp.einsum('btdl,dlc->btdc', ad, a_up.astype(f32)))
    g = jax.nn.sigmoid(gd) @ g_up.astype(f32)
    heads = lambda t: t.reshape(B, T, H, N)
    kk = heads(k * k_k.astype(f32))
    kk = kk / jnp.maximum(jnp.sqrt(jnp.sum(kk * kk, axis=-1, keepdims=True)), 1e-12)
    kk = kk.reshape(B, T, D_RWKV)
    rh, vh = heads(r), heads(v)
    out = jnp.zeros((B, T, H, N), f32)
    bonus = jnp.zeros((B, T, H, 1), f32)
    for d, rev in ((0, False), (1, True)):
        a_d = a[:, :, d]
        k_d = k * (1.0 + (a_d - 1.0) * k_a.astype(f32))
        out = out + wkv7_scan(rh, heads(decay[:, :, d]), heads(k_d), vh,
                              heads(-kk), heads(kk * a_d), rev)
        bonus = bonus + jnp.sum(rh * heads(k_d) * r_k.astype(f32), axis=-1, keepdims=True)
    mean = jnp.mean(out, axis=-1, keepdims=True)
    var = jnp.mean(jnp.square(out - mean), axis=-1, keepdims=True)
    out = ((out - mean) * lax.rsqrt(var + GN_EPS)).reshape(B, T, D_RWKV)
    out = out * gn_g.astype(f32) + gn_b.astype(f32)
    out = out + (bonus * vh).reshape(B, T, D_RWKV)
    return out * g


def neighbourhood_attention(q, k, v, rpb):
    B, T, _ = q.shape
    rows = T // GRID_W
    kh = min(WIN_H, rows)
    kw = WIN_W
    H, N = N_NAT_HEADS, HEAD_DIM
    f32 = jnp.float32
    grid = lambda t: t.reshape(B, rows, GRID_W, H, N).transpose(0, 3, 1, 2, 4)
    qg, kg, vg = grid(q), grid(k), grid(v)
    cols = np.arange(GRID_W)
    col_start = np.clip(cols - kw // 2, 0, GRID_W - kw)
    col_idx = col_start[:, None] + np.arange(kw)[None, :]
    dj = col_idx - cols[:, None] + (WIN_W - 1)
    row_start = np.clip(np.arange(rows) - kh // 2, 0, rows - kh).astype(np.int32)
    rpb_cols = rpb.astype(f32)[:, :, dj]
    scale = HEAD_DIM ** -0.5

    def one_row(args):
        i, rs, q_row = args
        k_rows = lax.dynamic_slice_in_dim(kg, rs, kh, axis=2)
        v_rows = lax.dynamic_slice_in_dim(vg, rs, kh, axis=2)
        k_win = k_rows[:, :, :, col_idx].astype(f32)
        v_win = v_rows[:, :, :, col_idx].astype(f32)
        di = rs + jnp.arange(kh) - i + (WIN_H - 1)
        bias = rpb_cols[:, di].transpose(0, 2, 1, 3)
        s = jnp.einsum('bhwn,bhrwcn->bhwrc', q_row.astype(f32), k_win) * scale + bias[None]
        pr = jax.nn.softmax(s.reshape(B, H, GRID_W, kh * kw), axis=-1).reshape(B, H, GRID_W, kh, kw)
        return jnp.einsum('bhwrc,bhrwcn->bhwn', pr, v_win).astype(q.dtype)

    out = lax.map(one_row, (jnp.arange(rows, dtype=jnp.int32), jnp.asarray(row_start),
                            jnp.moveaxis(qg, 2, 0)))
    return out.transpose(1, 0, 3, 2, 4).reshape(B, T, D_NAT)


def memory_cross_attention(h, mem, g_mem, w_q, w_kv, w_o):
    B, T, _ = h.shape
    M = mem.shape[1]
    q = (h @ w_q).reshape(B, T, N_XATTN_HEADS, XATTN_HEAD_DIM)
    k, v = jnp.split(rmsnorm(mem, g_mem) @ w_kv, 2, axis=-1)
    k = k.reshape(B, M, N_XATTN_HEADS, XATTN_HEAD_DIM)
    v = v.reshape(B, M, N_XATTN_HEADS, XATTN_HEAD_DIM)
    s = jnp.einsum('bthd,bmhd->bhtm', q.astype(jnp.float32), k.astype(jnp.float32)) * XATTN_HEAD_DIM ** -0.5
    pr = jax.nn.softmax(s, axis=-1)
    o = jnp.einsum('bhtm,bmhd->bthd', pr, v.astype(jnp.float32)).reshape(B, T, D_MODEL)
    return o.astype(h.dtype) @ w_o


def trunk(x, mem, norm_mix, w_in, mu_prev, mu_next, w0, w_up, a0, a_up, g_up, k_k, k_a, r_k,
          gn_g, gn_b, rpb, w_br_rwkv, w_br_nat, w_out, norm_x, norm_mem, w_xq, w_xkv, w_xo,
          norm_ff, w_ff1, w_ff2, norm_final):
    for l in range(DEPTH):
        p = rmsnorm(x, norm_mix[l]) @ w_in[l]
        p_rwkv = p[..., :RWKV_COLS]
        p_nat = p[..., RWKV_COLS:RWKV_COLS + NAT_COLS]
        gate_r, gate_n = jnp.split(p[..., RWKV_COLS + NAT_COLS:], 2, axis=-1)
        y_r = rwkv7_branch(p_rwkv, mu_prev[l], mu_next[l], w0[l], w_up[l], a0[l], a_up[l],
                           g_up[l], k_k[l], k_a[l], r_k[l], gn_g[l], gn_b[l]).astype(x.dtype)
        nq, nk, nv = jnp.split(p_nat, 3, axis=-1)
        y_n = neighbourhood_attention(nq, nk, nv, rpb[l])
        mixed = jax.nn.sigmoid(gate_r) * (y_r @ w_br_rwkv[l]) + jax.nn.sigmoid(gate_n) * (y_n @ w_br_nat[l])
        x = x + mixed @ w_out[l]
        x = x + memory_cross_attention(rmsnorm(x, norm_x[l]), mem, norm_mem[l], w_xq[l], w_xkv[l], w_xo[l])
        hf = rmsnorm(x, norm_ff[l]) @ w_ff1[l]
        x = x + jnp.square(jax.nn.relu(hf)) @ w_ff2[l]
    return rmsnorm(x, norm_final)


def setup_inputs(seed: int = 0) -> dict:
    key = jax.random.key(seed)
    ks = iter(jax.random.split(key, 40))
    nrm = lambda shape, s: jax.random.normal(next(ks), shape, jnp.float32) * s
    uni = lambda shape, lo, hi: jax.random.uniform(next(ks), shape, jnp.float32, lo, hi)
    L, D = DEPTH, D_MODEL
    return {
        "x_prompt": nrm((BATCH, SEQ, D), 1.0),
        "x_sample": nrm((DEC_BATCH, DEC_SEQ, D), 1.0),
        "mem_prompt": nrm((BATCH, N_MEM, D), 1.0),
        "mem_sample": nrm((DEC_BATCH, N_MEM, D), 1.0),
        "norm_mix": 1.0 + nrm((L, D), 0.05),
        "w_in": nrm((L, D, D_IN), D ** -0.5),
        "mu_prev": uni((L, RWKV_COLS), 0.0, 0.5),
        "mu_next": uni((L, RWKV_COLS), 0.0, 0.5),
        "w0": uni((L, N_DIR, D_RWKV), -3.0, 1.0),
        "w_up": nrm((L, N_DIR, LORA_W, D_RWKV), 0.1 * LORA_W ** -0.5),
        "a0": nrm((L, N_DIR, D_RWKV), 0.1),
        "a_up": nrm((L, N_DIR, LORA_A, D_RWKV), 0.5 * LORA_A ** -0.5),
        "g_up": nrm((L, LORA_G, D_RWKV), LORA_G ** -0.5),
        "k_k": 0.85 + nrm((L, D_RWKV), 0.05),
        "k_a": 1.0 + nrm((L, D_RWKV), 0.05),
        "r_k": nrm((L, N_RWKV_HEADS, HEAD_DIM), 0.1),
        "gn_g": 1.0 + nrm((L, D_RWKV), 0.05),
        "gn_b": nrm((L, D_RWKV), 0.01),
        "rpb": nrm((L, N_NAT_HEADS, 2 * WIN_H - 1, 2 * WIN_W - 1), 0.1),
        "w_br_rwkv": nrm((L, D_RWKV, D), D_RWKV ** -0.5),
        "w_br_nat": nrm((L, D_NAT, D), D_NAT ** -0.5),
        "w_out": nrm((L, D, D), D ** -0.5),
        "norm_x": 1.0 + nrm((L, D), 0.05),
        "norm_mem": 1.0 + nrm((L, D), 0.05),
        "w_xq": nrm((L, D, D), D ** -0.5),
        "w_xkv": nrm((L, D, 2 * D), D ** -0.5),
        "w_xo": nrm((L, D, D), D ** -0.5),
        "norm_ff": 1.0 + nrm((L, D), 0.05),
        "w_ff1": nrm((L, D, D_FF), D ** -0.5),
        "w_ff2": nrm((L, D_FF, D), D_FF ** -0.5),
        "norm_final": 1.0 + nrm((D,), 0.05),
    }


def reference(x_prompt, x_sample, mem_prompt, mem_sample, norm_mix, w_in, mu_prev, mu_next, w0, w_up,
              a0, a_up, g_up, k_k, k_a, r_k, gn_g, gn_b, rpb, w_br_rwkv, w_br_nat, w_out, norm_x,
              norm_mem, w_xq, w_xkv, w_xo, norm_ff, w_ff1, w_ff2, norm_final):
    y_prompt = trunk(x_prompt, mem_prompt, norm_mix, w_in, mu_prev, mu_next, w0, w_up, a0, a_up, g_up,
                     k_k, k_a, r_k, gn_g, gn_b, rpb, w_br_rwkv, w_br_nat, w_out, norm_x, norm_mem,
                     w_xq, w_xkv, w_xo, norm_ff, w_ff1, w_ff2, norm_final)
    y_sample = trunk(x_sample, mem_sample, norm_mix, w_in, mu_prev, mu_next, w0, w_up, a0, a_up, g_up,
                     k_k, k_a, r_k, gn_g, gn_b, rpb, w_br_rwkv, w_br_nat, w_out, norm_x, norm_mem,
                     w_xq, w_xkv, w_xo, norm_ff, w_ff1, w_ff2, norm_final)
    return (y_prompt, y_sample)
```

```python
import functools
import math

import numpy as np
import jax
import jax.numpy as jnp
from jax import lax
from jax.experimental import pallas as pl
from jax.experimental.pallas import tpu as pltpu

F32 = jnp.float32
BF16 = jnp.bfloat16

D_MODEL = 1024
GRID_W = 64
HEAD_DIM = 64
D_RWKV = 512
D_NAT = 512
N_NAT_HEADS = D_NAT // HEAD_DIM
LORA_W = 64
LORA_A = 64
LORA_G = 128
WIN_H = 8
WIN_W = 16
N_XATTN_HEADS = 4
XATTN_HEAD_DIM = D_MODEL // N_XATTN_HEADS
D_FF = 4 * D_MODEL
NORM_EPS = 1e-6
GN_EPS = 1e-5 * HEAD_DIM
RWKV_COLS = 3 * D_RWKV + 2 * LORA_W + 2 * LORA_A + LORA_G
NAT_COLS = 3 * D_NAT
GATE_COLS = 2 * D_MODEL
COL_WD = 3 * D_RWKV
COL_AD = COL_WD + 2 * LORA_W
COL_GD = COL_AD + 2 * LORA_A

LANES = 128
N_PAIRS = D_RWKV // LANES
CHUNK = 64
WIN_TOKENS = WIN_H * GRID_W
NEG_BIG = -1e30
VMEM_LIMIT = 56 * 1024 * 1024


def _dot(a, b):
    return jnp.dot(a, b, preferred_element_type=F32)


def _dot_nt(a, b):
    return lax.dot_general(a, b, (((1,), (1,)), ((), ())), preferred_element_type=F32)


def _dot_tn(a, b):
    return lax.dot_general(a, b, (((0,), (0,)), ((), ())), preferred_element_type=F32)


def _dot_split(a, b):
    hi = a.astype(BF16)
    lo = (a - hi.astype(F32)).astype(BF16)
    return _dot(hi, b) + _dot(lo, b)


def _dot_split_lhs(a, b):
    hi = b.astype(BF16)
    lo = (b - hi.astype(F32)).astype(BF16)
    return _dot(a, hi) + _dot(a, lo)


def _sigmoid(x):
    return 1.0 / (1.0 + jnp.exp(-x))


def _rmsnorm(x, g):
    return x * lax.rsqrt(jnp.mean(x * x, axis=-1, keepdims=True) + NORM_EPS) * g


def _params(sem):
    return pltpu.CompilerParams(dimension_semantics=sem, vmem_limit_bytes=VMEM_LIMIT)


def _full(shape):
    nd = len(shape)
    return pl.BlockSpec(shape, lambda *_: (0,) * nd)


def _norm_proj_kernel(n_out, x_ref, g_ref, *refs):
    w_refs, o_refs = refs[:n_out], refs[n_out:]
    h = _rmsnorm(x_ref[...], g_ref[...]).astype(BF16)
    for w_ref, o_ref in zip(w_refs, o_refs):
        o_ref[...] = _dot(h, w_ref[...]).astype(o_ref.dtype)


def _norm_proj(x2d, g, ws, out_dtypes, tm, name):
    n, d = x2d.shape
    n_out = len(ws)
    return pl.pallas_call(
        functools.partial(_norm_proj_kernel, n_out),
        out_shape=[jax.ShapeDtypeStruct((n, w.shape[1]), dt) for w, dt in zip(ws, out_dtypes)],
        grid=(n // tm,),
        in_specs=[pl.BlockSpec((tm, d), lambda i: (i, 0)), _full((1, d))]
        + [_full(w.shape) for w in ws],
        out_specs=[pl.BlockSpec((tm, w.shape[1]), lambda i: (i, 0)) for w in ws],
        compiler_params=_params(("parallel",)),
        name=name,
    )(x2d, g.reshape(1, d), *ws)


def _wkv_kernel(reverse, n_chunks, *refs):
    if reverse:
        (p_ref, pprev_ref, pnext_ref, mup_ref, mun_ref, w0_ref, wup_ref, a0_ref, aup_ref,
         kk_ref, ka_ref, seg_ref, yf_ref, a0o_ref, aupo_ref, gup_ref, rk_ref, gng_ref, gnb_ref,
         out_ref, s_ref) = refs
    else:
        (p_ref, pprev_ref, pnext_ref, mup_ref, mun_ref, w0_ref, wup_ref, a0_ref, aup_ref,
         kk_ref, ka_ref, seg_ref, out_ref, s_ref) = refs

    c = pl.program_id(1)
    tb = (n_chunks - 1 - c) if reverse else c

    @pl.when(c == 0)
    def _():
        s_ref[...] = jnp.zeros_like(s_ref)

    p = p_ref[0]
    row = lax.broadcasted_iota(jnp.int32, (CHUNK, 1), 0)
    prev_row = jnp.where(tb > 0, pprev_ref[0, 7:8, :], 0.0)
    next_row = jnp.where(tb < n_chunks - 1, pnext_ref[0, 0:1, :], 0.0)
    p_prev = jnp.where(row == 0, prev_row, pltpu.roll(p, 1, 0))
    p_next = jnp.where(row == CHUNK - 1, next_row, pltpu.roll(p, CHUNK - 1, 0))
    ps = p + mup_ref[...] * (p_prev - p) + mun_ref[...] * (p_next - p)

    r = ps[:, 0:D_RWKV]
    k = ps[:, D_RWKV:2 * D_RWKV]
    v = ps[:, 2 * D_RWKV:3 * D_RWKV]
    wd = ps[:, COL_WD:COL_AD]
    ad = ps[:, COL_AD:COL_GD]
    seg = seg_ref[...]

    w_raw = w0_ref[...] + _dot(jnp.tanh(wd).astype(BF16), wup_ref[...])
    logw = (-math.exp(-0.5)) * _sigmoid(w_raw)
    ad_b = ad.astype(BF16)
    a_d = _sigmoid(a0_ref[...] + _dot(ad_b, aup_ref[...]))
    kkv = k * kk_ref[...]
    kkn = kkv / jnp.maximum(jnp.sqrt(_dot_split(kkv * kkv, seg)), 1e-12)
    k_d = k * (1.0 + (a_d - 1.0) * ka_ref[...])
    b_vec = kkn * a_d

    ti = lax.broadcasted_iota(jnp.int32, (CHUNK, CHUNK), 0)
    si = lax.broadcasted_iota(jnp.int32, (CHUNK, CHUNK), 1)
    tri = ((si >= ti) if reverse else (si <= ti)).astype(BF16)
    cum = _dot_split_lhs(tri, logw)
    total = cum[0:1] if reverse else cum[CHUNK - 1:CHUNK]
    e_in = jnp.exp(cum)
    a_t = (-kkn) * jnp.exp(cum - logw)
    r_t = r * e_in
    e_inv = jnp.exp(-cum)
    b_t = b_vec * e_inv
    k_t = k_d * e_inv
    e_out = jnp.exp(total - cum)
    b_o = b_vec * e_out
    k_o = k_d * e_out
    p_tot = jnp.exp(total)

    ri = lax.broadcasted_iota(jnp.int32, (2 * CHUNK, LANES), 0)
    li = lax.broadcasted_iota(jnp.int32, (2 * CHUNK, LANES), 1)
    t_idx = ri % CHUNK
    s_idx = li % CHUNK
    incl = (ri >= CHUNK).astype(jnp.int32)
    sc_mask = (s_idx > t_idx - incl) if reverse else (s_idx < t_idx + incl)
    lane = lax.broadcasted_iota(jnp.int32, (1, LANES), 1)
    head_lanes = (lane < HEAD_DIM, lane >= HEAD_DIM)
    bd_mask = (ri // HEAD_DIM) == (li // HEAD_DIM)
    zeros_cv = jnp.zeros((CHUNK, LANES), BF16)

    ys = []
    for g in range(N_PAIRS):
        sl = slice(g * LANES, (g + 1) * LANES)
        lhs_f = jnp.concatenate([a_t[:, sl], r_t[:, sl]], axis=0)
        lhs = lhs_f.astype(BF16)
        rhs = jnp.concatenate([b_t[:, sl], k_t[:, sl]], axis=0).astype(BF16)
        v_g = v[:, sl].astype(BF16)
        state = s_ref[g]
        from_state = _dot_nt(lhs, state.astype(BF16))
        zv = jnp.concatenate([zeros_cv, v_g], axis=0)

        tops, bots, akv = [], [], []
        for e in range(2):
            sc = _dot_nt(jnp.where(head_lanes[e], lhs_f, 0.0).astype(BF16), rhs)
            sc = jnp.where(sc_mask, sc, 0.0)
            tops.append(sc[:CHUNK])
            bots.append(sc[CHUNK:].astype(BF16))
            akv.append(_dot(sc[:CHUNK].astype(BF16), zv))
        x0 = from_state[:CHUNK] + jnp.where(head_lanes[0], akv[0], akv[1])

        us = []
        for e in range(2):
            a_pad = jnp.where(head_lanes[0], tops[e], 0.0)
            x = x0
            n_steps = int(math.log2(CHUNK))
            for it in range(n_steps):
                a_sq = a_pad[:, :CHUNK].astype(BF16)
                if it < n_steps - 1:
                    z = jnp.concatenate([a_pad, x], axis=1).astype(BF16)
                    rz = _dot(a_sq, z)
                    a_pad = rz[:, :LANES]
                    x = x + rz[:, LANES:]
                else:
                    x = x + _dot(a_sq, x.astype(BF16))
            us.append(x)
        u_g = jnp.where(head_lanes[0], us[0], us[1])
        uv = jnp.concatenate([u_g.astype(BF16), v_g], axis=0)
        y_g = from_state[CHUNK:] + jnp.where(head_lanes[0], _dot(bots[0], uv), _dot(bots[1], uv))
        ys.append(y_g)

        bk_o = jnp.concatenate([b_o[:, sl], k_o[:, sl]], axis=0).astype(BF16)
        upd = _dot_tn(uv, bk_o)
        s_ref[g] = state * p_tot[:, sl] + jnp.where(bd_mask, upd, 0.0)

    y = jnp.concatenate(ys, axis=1)
    if not reverse:
        out_ref[0] = y
        return

    out = yf_ref[0] + y
    inv_n = 1.0 / HEAD_DIM
    mean = _dot_split(out, seg) * inv_n
    cen = out - mean
    var = _dot_split(cen * cen, seg) * inv_n
    out = cen * lax.rsqrt(var + GN_EPS) * gng_ref[...] + gnb_ref[...]
    a_o = _sigmoid(a0o_ref[...] + _dot(ad_b, aupo_ref[...]))
    k_sum = k_d + k * (1.0 + (a_o - 1.0) * ka_ref[...])
    bonus = _dot_split(r * k_sum * rk_ref[...], seg)
    out = out + bonus * v
    gd = ps[:, COL_GD:RWKV_COLS]
    gate = _dot(_sigmoid(gd).astype(BF16), gup_ref[...])
    out_ref[0] = (out * gate).astype(out_ref.dtype)


def _wkv_call(reverse, p_rw, shared, dir_params, extra):
    b, t, _ = p_rw.shape
    n_chunks = t // CHUNK
    rows8 = t // 8
    per8 = CHUNK // 8

    def tb_of(c):
        return (n_chunks - 1 - c) if reverse else c

    cur = lambda bi, c: (bi, tb_of(c), 0)
    prev = lambda bi, c: (bi, jnp.maximum(tb_of(c) * per8 - 1, 0), 0)
    nxt = lambda bi, c: (bi, jnp.minimum((tb_of(c) + 1) * per8, rows8 - 1), 0)

    args = [p_rw, p_rw, p_rw] + list(shared[:2]) + list(dir_params) + list(shared[2:])
    in_specs = [pl.BlockSpec((1, CHUNK, RWKV_COLS), cur),
                pl.BlockSpec((1, 8, RWKV_COLS), prev),
                pl.BlockSpec((1, 8, RWKV_COLS), nxt)]
    in_specs += [_full(a.shape) for a in args[3:]]
    if reverse:
        y_f = extra[0]
        args += [y_f] + list(extra[1:])
        in_specs += [pl.BlockSpec((1, CHUNK, D_RWKV), cur)] + [_full(a.shape) for a in extra[1:]]
        out_dtype = BF16
    else:
        out_dtype = F32
    return pl.pallas_call(
        functools.partial(_wkv_kernel, reverse, n_chunks),
        out_shape=jax.ShapeDtypeStruct((b, t, D_RWKV), out_dtype),
        grid=(b, n_chunks),
        in_specs=in_specs,
        out_specs=pl.BlockSpec((1, CHUNK, D_RWKV), cur),
        scratch_shapes=[pltpu.VMEM((N_PAIRS, LANES, LANES), F32)],
        compiler_params=_params(("parallel", "arbitrary")),
        name="wkv_bwd" if reverse else "wkv_fwd",
    )(*args)


def _nat_kernel(n_rows, q_ref, k_ref, v_ref, bias_ref, out_ref):
    i = pl.program_id(1)
    rs = jnp.clip(i - WIN_H // 2, 0, n_rows - WIN_H)
    start = pl.multiple_of(rs * GRID_W, GRID_W)
    lane = lax.broadcasted_iota(jnp.int32, (1, LANES), 1)
    head_lanes = (lane < HEAD_DIM, lane >= HEAD_DIM)
    scale = HEAD_DIM ** -0.5
    for g in range(N_PAIRS):
        sl = slice(g * LANES, (g + 1) * LANES)
        kw = k_ref[0, pl.ds(start, WIN_TOKENS), sl]
        vw = v_ref[0, pl.ds(start, WIN_TOKENS), sl]
        qg = q_ref[0, :, sl].astype(F32) * scale
        outs = []
        for e in range(2):
            qe = jnp.where(head_lanes[e], qg, 0.0).astype(BF16)
            s = _dot_nt(qe, kw) + bias_ref[2 * g + e, 0].astype(F32)
            m = jnp.max(s, axis=-1, keepdims=True)
            pr = jnp.exp(s - m)
            l = jnp.sum(pr, axis=-1, keepdims=True)
            outs.append(_dot(pr.astype(BF16), vw) / l)
        out_ref[0, :, sl] = jnp.where(head_lanes[0], outs[0], outs[1]).astype(out_ref.dtype)


def _nat_bias_table(rpb):
    c = np.arange(GRID_W)[:, None]
    cp = np.arange(GRID_W)[None, :]
    cs = np.clip(c - WIN_W // 2, 0, GRID_W - WIN_W)
    valid = (cp >= cs) & (cp < cs + WIN_W)
    dj = np.clip(cp - c + (WIN_W - 1), 0, 2 * WIN_W - 2)
    di = np.arange(WIN_H)[None, :] - np.arange(WIN_H)[:, None] + (WIN_H - 1)
    tab = rpb.astype(F32)[:, di][:, :, :, dj]
    tab = jnp.where(valid[None, None, None], tab, NEG_BIG)
    tab = tab.transpose(0, 1, 3, 2, 4).reshape(N_NAT_HEADS, WIN_H, GRID_W, WIN_TOKENS)
    return tab.astype(BF16)


def _nat_call(p_nat, bias_tab):
    b, t, _ = p_nat.shape
    n_rows = t // GRID_W
    half = WIN_H // 2

    def delta_of(i):
        return i - jnp.clip(i - half, 0, n_rows - WIN_H)

    return pl.pallas_call(
        functools.partial(_nat_kernel, n_rows),
        out_shape=jax.ShapeDtypeStruct((b, t, D_NAT), BF16),
        grid=(b, n_rows),
        in_specs=[pl.BlockSpec((1, GRID_W, D_NAT), lambda bi, i: (bi, i, 0)),
                  pl.BlockSpec((1, t, D_NAT), lambda bi, i: (bi, 0, 1)),
                  pl.BlockSpec((1, t, D_NAT), lambda bi, i: (bi, 0, 2)),
                  pl.BlockSpec((N_NAT_HEADS, 1, GRID_W, WIN_TOKENS),
                               lambda bi, i: (0, delta_of(i), 0, 0))],
        out_specs=pl.BlockSpec((1, GRID_W, D_NAT), lambda bi, i: (bi, i, 0)),
        compiler_params=_params(("parallel", "arbitrary")),
        name="nat",
    )(p_nat, p_nat, p_nat, bias_tab)


def _merge_xattn_kernel(x_ref, yr_ref, yn_ref, gate_ref, kv_ref, wbr_ref, wbn_ref, wout_ref,
                        gx_ref, wq_ref, wo_ref, out_ref):
    gates = gate_ref[0].astype(F32)
    mixed = (_sigmoid(gates[:, :D_MODEL]) * _dot(yr_ref[0], wbr_ref[...])
             + _sigmoid(gates[:, D_MODEL:]) * _dot(yn_ref[0], wbn_ref[...]))
    x1 = x_ref[0] + _dot(mixed.astype(BF16), wout_ref[...])
    q = _dot(_rmsnorm(x1, gx_ref[...]).astype(BF16), wq_ref[...])
    scale = XATTN_HEAD_DIM ** -0.5
    heads = []
    for h in range(N_XATTN_HEADS):
        sl = slice(h * XATTN_HEAD_DIM, (h + 1) * XATTN_HEAD_DIM)
        qh = (q[:, sl] * scale).astype(BF16)
        kh = kv_ref[0, :, sl]
        vh = kv_ref[0, :, D_MODEL + h * XATTN_HEAD_DIM:D_MODEL + (h + 1) * XATTN_HEAD_DIM]
        s = _dot_nt(qh, kh)
        m = jnp.max(s, axis=-1, keepdims=True)
        pr = jnp.exp(s - m)
        l = jnp.sum(pr, axis=-1, keepdims=True)
        heads.append(_dot(pr.astype(BF16), vh) / l)
    o = jnp.concatenate(heads, axis=1).astype(BF16)
    out_ref[0] = x1 + _dot(o, wo_ref[...])


def _merge_xattn_call(x, y_r, y_n, gates, kv, wbr, wbn, wout, gx, wq, wo, tm):
    b, t, d = x.shape
    n_mem = kv.shape[1]
    tile = lambda w: pl.BlockSpec((1, tm, w), lambda bi, i: (bi, i, 0))
    ws = [wbr, wbn, wout, gx.reshape(1, d), wq, wo]
    return pl.pallas_call(
        _merge_xattn_kernel,
        out_shape=jax.ShapeDtypeStruct((b, t, d), F32),
        grid=(b, t // tm),
        in_specs=[tile(d), tile(D_RWKV), tile(D_NAT), tile(GATE_COLS),
                  pl.BlockSpec((1, n_mem, 2 * d), lambda bi, i: (bi, 0, 0))]
        + [_full(w.shape) for w in ws],
        out_specs=tile(d),
        compiler_params=_params(("parallel", "arbitrary")),
        name="merge_xattn",
    )(x, y_r, y_n, gates, kv, *ws)


def _ffn_kernel(final, ff_chunk, x_ref, g_ref, w1_ref, w2_ref, gf_ref, out_ref):
    x = x_ref[...]
    h = _rmsnorm(x, g_ref[...]).astype(BF16)
    acc = x
    for j in range(D_FF // ff_chunk):
        sl = slice(j * ff_chunk, (j + 1) * ff_chunk)
        hf = jnp.maximum(_dot(h, w1_ref[:, sl]), 0.0)
        acc = acc + _dot((hf * hf).astype(BF16), w2_ref[sl, :])
    if final:
        acc = _rmsnorm(acc, gf_ref[...])
    out_ref[...] = acc


def _ffn_call(x2d, g, w1, w2, g_final, final, tm):
    n, d = x2d.shape
    single = pl.Buffered(1)
    return pl.pallas_call(
        functools.partial(_ffn_kernel, final, 1024),
        out_shape=jax.ShapeDtypeStruct((n, d), F32),
        grid=(n // tm,),
        in_specs=[pl.BlockSpec((tm, d), lambda i: (i, 0)), _full((1, d)),
                  pl.BlockSpec(w1.shape, lambda i: (0, 0), pipeline_mode=single),
                  pl.BlockSpec(w2.shape, lambda i: (0, 0), pipeline_mode=single),
                  _full((1, d))],
        out_specs=pl.BlockSpec((tm, d), lambda i: (i, 0)),
        compiler_params=_params(("parallel",)),
        name="ffn",
    )(x2d, g.reshape(1, d), w1, w2, g_final.reshape(1, d))


def _pad_dir(w, d):
    z = jnp.zeros_like(w[d])
    parts = [w[0], z] if d == 0 else [z, w[1]]
    return jnp.concatenate(parts, axis=0).astype(BF16)


def _trunk(x, mem, norm_mix, w_in, mu_prev, mu_next, w0, w_up, a0, a_up, g_up, k_k, k_a, r_k,
           gn_g, gn_b, rpb, w_br_rwkv, w_br_nat, w_out, norm_x, norm_mem, w_xq, w_xkv, w_xo,
           norm_ff, w_ff1, w_ff2, norm_final, tm=256):
    b, t, d = x.shape
    n_mem = mem.shape[1]
    depth = w_in.shape[0]
    head_id = np.arange(D_RWKV) // HEAD_DIM
    seg = jnp.asarray(head_id[:, None] == head_id[None, :], dtype=BF16)
    row = lambda a: a.reshape(1, -1).astype(F32)
    mem2d = mem.reshape(b * n_mem, d)
    for l in range(depth):
        w_in_l = w_in[l].astype(BF16)
        p_rw, p_nat, p_gate = _norm_proj(
            x.reshape(b * t, d), norm_mix[l],
            [w_in_l[:, :RWKV_COLS], w_in_l[:, RWKV_COLS:RWKV_COLS + NAT_COLS],
             w_in_l[:, RWKV_COLS + NAT_COLS:]],
            [F32, BF16, BF16], tm, "in_proj")
        p_rw = p_rw.reshape(b, t, RWKV_COLS)
        p_nat = p_nat.reshape(b, t, NAT_COLS)
        p_gate = p_gate.reshape(b, t, GATE_COLS)

        shared = [row(mu_prev[l]), row(mu_next[l]), row(k_k[l]), row(k_a[l]), seg]
        dir_params = lambda dd: [row(w0[l, dd]), _pad_dir(w_up[l], dd), row(a0[l, dd]),
                                 _pad_dir(a_up[l], dd)]
        y_f = _wkv_call(False, p_rw, shared, dir_params(0), None)
        extra = [y_f, row(a0[l, 0]), _pad_dir(a_up[l], 0), g_up[l].astype(BF16), row(r_k[l]),
                 row(gn_g[l]), row(gn_b[l])]
        y_r = _wkv_call(True, p_rw, shared, dir_params(1), extra)

        y_n = _nat_call(p_nat, _nat_bias_table(rpb[l]))

        (kv,) = _norm_proj(mem2d, norm_mem[l], [w_xkv[l].astype(BF16)], [BF16], tm, "mem_kv")
        kv = kv.reshape(b, n_mem, 2 * d)
        x = _merge_xattn_call(x, y_r, y_n, p_gate, kv, w_br_rwkv[l].astype(BF16),
                              w_br_nat[l].astype(BF16), w_out[l].astype(BF16), norm_x[l],
                              w_xq[l].astype(BF16), w_xo[l].astype(BF16), tm)
        x = _ffn_call(x.reshape(b * t, d), norm_ff[l], w_ff1[l].astype(BF16),
                      w_ff2[l].astype(BF16), norm_final, l == depth - 1, tm).reshape(b, t, d)
    return x


def kernel(x_prompt, x_sample, mem_prompt, mem_sample, norm_mix, w_in, mu_prev, mu_next, w0, w_up, a0, a_up, g_up, k_k, k_a, r_k, gn_g, gn_b, rpb, w_br_rwkv, w_br_nat, w_out, norm_x, norm_mem, w_xq, w_xkv, w_xo, norm_ff, w_ff1, w_ff2, norm_final):
    assert x_prompt.shape[1:] == x_sample.shape[1:] and mem_prompt.shape[1:] == mem_sample.shape[1:]
    nb = x_prompt.shape[0]
    x = jnp.concatenate([x_prompt, x_sample], axis=0)
    mem = jnp.concatenate([mem_prompt, mem_sample], axis=0)
    y = _trunk(x, mem, norm_mix, w_in, mu_prev, mu_next, w0, w_up, a0, a_up, g_up, k_k, k_a,
               r_k, gn_g, gn_b, rpb, w_br_rwkv, w_br_nat, w_out, norm_x, norm_mem, w_xq, w_xkv,
               w_xo, norm_ff, w_ff1, w_ff2, norm_final)
    return (y[:nb], y[nb:])
```

```python
import functools
import math

import numpy as np
import jax
import jax.numpy as jnp
from jax import lax
from jax.experimental import pallas as pl
from jax.experimental.pallas import tpu as pltpu

F32 = jnp.float32
BF16 = jnp.bfloat16

D_MODEL = 1024
GRID_W = 64
HEAD_DIM = 64
D_RWKV = 512
D_NAT = 512
N_NAT_HEADS = D_NAT // HEAD_DIM
LORA_W = 64
LORA_A = 64
LORA_G = 128
WIN_H = 8
WIN_W = 16
N_XATTN_HEADS = 4
XATTN_HEAD_DIM = D_MODEL // N_XATTN_HEADS
D_FF = 4 * D_MODEL
NORM_EPS = 1e-6
GN_EPS = 1e-5 * HEAD_DIM
RWKV_COLS = 3 * D_RWKV + 2 * LORA_W + 2 * LORA_A + LORA_G
NAT_COLS = 3 * D_NAT
GATE_COLS = 2 * D_MODEL
COL_WD = 3 * D_RWKV
COL_AD = COL_WD + 2 * LORA_W
COL_GD = COL_AD + 2 * LORA_A

LANES = 128
N_PAIRS = D_RWKV // LANES
CHUNK = 64
WKV_SEQS = 2
WIN_TOKENS = WIN_H * GRID_W
NEG_BIG = -1e30
VMEM_LIMIT = 56 * 1024 * 1024


def _dot(a, b):
    return jnp.dot(a, b, preferred_element_type=F32)


def _dot_nt(a, b):
    return lax.dot_general(a, b, (((1,), (1,)), ((), ())), preferred_element_type=F32)


def _dot_tn(a, b):
    return lax.dot_general(a, b, (((0,), (0,)), ((), ())), preferred_element_type=F32)


def _dot_split(a, b):
    hi = a.astype(BF16)
    lo = (a - hi.astype(F32)).astype(BF16)
    return _dot(hi, b) + _dot(lo, b)


def _dot_split_lhs(a, b):
    hi = b.astype(BF16)
    lo = (b - hi.astype(F32)).astype(BF16)
    return _dot(a, hi) + _dot(a, lo)


def _sigmoid(x):
    return 1.0 / (1.0 + jnp.exp(-x))


def _rmsnorm(x, g):
    return x * lax.rsqrt(jnp.mean(x * x, axis=-1, keepdims=True) + NORM_EPS) * g


def _params(sem):
    return pltpu.CompilerParams(dimension_semantics=sem, vmem_limit_bytes=VMEM_LIMIT)


def _full(shape):
    nd = len(shape)
    return pl.BlockSpec(shape, lambda *_: (0,) * nd)


def _norm_proj_kernel(n_out, x_ref, g_ref, *refs):
    w_refs, o_refs = refs[:n_out], refs[n_out:]
    h = _rmsnorm(x_ref[...], g_ref[...]).astype(BF16)
    for w_ref, o_ref in zip(w_refs, o_refs):
        o_ref[...] = _dot(h, w_ref[...]).astype(o_ref.dtype)


def _norm_proj(x2d, g, ws, out_dtypes, tm, name):
    n, d = x2d.shape
    n_out = len(ws)
    return pl.pallas_call(
        functools.partial(_norm_proj_kernel, n_out),
        out_shape=[jax.ShapeDtypeStruct((n, w.shape[1]), dt) for w, dt in zip(ws, out_dtypes)],
        grid=(n // tm,),
        in_specs=[pl.BlockSpec((tm, d), lambda i: (i, 0)), _full((1, d))]
        + [_full(w.shape) for w in ws],
        out_specs=[pl.BlockSpec((tm, w.shape[1]), lambda i: (i, 0)) for w in ws],
        compiler_params=_params(("parallel",)),
        name=name,
    )(x2d, g.reshape(1, d), *ws)


def _wkv_kernel(reverse, n_chunks, nb, *refs):
    if reverse:
        (p_ref, pprev_ref, pnext_ref, mup_ref, mun_ref, w0_ref, wup_ref, a0_ref, aup_ref,
         kk_ref, ka_ref, seg_ref, yf_ref, a0o_ref, aupo_ref, gup_ref, rk_ref, gng_ref, gnb_ref,
         out_ref, s_ref) = refs
    else:
        (p_ref, pprev_ref, pnext_ref, mup_ref, mun_ref, w0_ref, wup_ref, a0_ref, aup_ref,
         kk_ref, ka_ref, seg_ref, out_ref, s_ref) = refs

    c = pl.program_id(1)
    tb = (n_chunks - 1 - c) if reverse else c
    rows = nb * CHUNK

    @pl.when(c == 0)
    def _():
        s_ref[...] = jnp.zeros_like(s_ref)

    row = lax.broadcasted_iota(jnp.int32, (CHUNK, 1), 0)
    mup, mun = mup_ref[...], mun_ref[...]
    shifted = []
    for i in range(nb):
        p = p_ref[i]
        prev_row = jnp.where(tb > 0, pprev_ref[i, 7:8, :], 0.0)
        next_row = jnp.where(tb < n_chunks - 1, pnext_ref[i, 0:1, :], 0.0)
        p_prev = jnp.where(row == 0, prev_row, pltpu.roll(p, 1, 0))
        p_next = jnp.where(row == CHUNK - 1, next_row, pltpu.roll(p, CHUNK - 1, 0))
        shifted.append(p + mup * (p_prev - p) + mun * (p_next - p))
    ps = jnp.concatenate(shifted, axis=0) if nb > 1 else shifted[0]

    r = ps[:, 0:D_RWKV]
    k = ps[:, D_RWKV:2 * D_RWKV]
    v = ps[:, 2 * D_RWKV:3 * D_RWKV]
    wd = ps[:, COL_WD:COL_AD]
    ad = ps[:, COL_AD:COL_GD]
    seg = seg_ref[...]

    w_raw = w0_ref[...] + _dot(jnp.tanh(wd).astype(BF16), wup_ref[...])
    logw = (-math.exp(-0.5)) * _sigmoid(w_raw)
    ad_b = ad.astype(BF16)
    a_d = _sigmoid(a0_ref[...] + _dot(ad_b, aup_ref[...]))
    kkv = k * kk_ref[...]
    kkn = kkv / jnp.maximum(jnp.sqrt(_dot_split(kkv * kkv, seg)), 1e-12)
    k_d = k * (1.0 + (a_d - 1.0) * ka_ref[...])
    b_vec = kkn * a_d

    ti = lax.broadcasted_iota(jnp.int32, (CHUNK, CHUNK), 0)
    si = lax.broadcasted_iota(jnp.int32, (CHUNK, CHUNK), 1)
    tri = ((si >= ti) if reverse else (si <= ti)).astype(BF16)
    last = 0 if reverse else CHUNK - 1
    cums, totals = [], []
    for i in range(nb):
        cum_i = _dot_split_lhs(tri, logw[i * CHUNK:(i + 1) * CHUNK])
        cums.append(cum_i)
        totals.append(cum_i[last:last + 1])
    cum = jnp.concatenate(cums, axis=0) if nb > 1 else cums[0]
    total = (jnp.concatenate([jnp.broadcast_to(tt, (CHUNK, D_RWKV)) for tt in totals], axis=0)
             if nb > 1 else totals[0])
    a_t = (-kkn) * jnp.exp(cum - logw)
    r_t = r * jnp.exp(cum)
    e_inv = jnp.exp(-cum)
    b_t = b_vec * e_inv
    k_t = k_d * e_inv
    e_out = jnp.exp(total - cum)
    b_o = b_vec * e_out
    k_o = k_d * e_out

    ri = lax.broadcasted_iota(jnp.int32, (2 * CHUNK, LANES), 0)
    li = lax.broadcasted_iota(jnp.int32, (2 * CHUNK, LANES), 1)
    t_idx = ri % CHUNK
    s_idx = li % CHUNK
    incl = (ri >= CHUNK).astype(jnp.int32)
    sc_mask = (s_idx > t_idx - incl) if reverse else (s_idx < t_idx + incl)
    lane = lax.broadcasted_iota(jnp.int32, (1, LANES), 1)
    head_lanes = (lane < HEAD_DIM, lane >= HEAD_DIM)
    bd_mask = (ri // HEAD_DIM) == (li // HEAD_DIM)
    zeros_cv = jnp.zeros((CHUNK, LANES), BF16)

    tasks = [(i, g) for i in range(nb) for g in range(N_PAIRS)]
    n_tasks = len(tasks)
    n_steps = int(math.log2(CHUNK))

    def tile(a, i, g):
        return a[i * CHUNK:(i + 1) * CHUNK, g * LANES:(g + 1) * LANES]

    lhs_f = [jnp.concatenate([tile(a_t, i, g), tile(r_t, i, g)], axis=0) for i, g in tasks]
    rhs = [jnp.concatenate([tile(b_t, i, g), tile(k_t, i, g)], axis=0).astype(BF16)
           for i, g in tasks]
    v_g = [tile(v, i, g).astype(BF16) for i, g in tasks]
    state = [s_ref[i, g] for i, g in tasks]
    from_state = [_dot_nt(lf.astype(BF16), st.astype(BF16)) for lf, st in zip(lhs_f, state)]
    sc = [[jnp.where(sc_mask, _dot_nt(jnp.where(head_lanes[e], lf, 0.0).astype(BF16), rh), 0.0)
           for e in range(2)] for lf, rh in zip(lhs_f, rhs)]
    zv = [jnp.concatenate([zeros_cv, vg], axis=0) for vg in v_g]
    akv = [[_dot(sc[n][e][:CHUNK].astype(BF16), zv[n]) for e in range(2)] for n in range(n_tasks)]
    x0 = [from_state[n][:CHUNK] + jnp.where(head_lanes[0], akv[n][0], akv[n][1])
          for n in range(n_tasks)]

    a_pad = [[jnp.where(head_lanes[0], sc[n][e][:CHUNK], 0.0) for e in range(2)]
             for n in range(n_tasks)]
    xs = [[x0[n], x0[n]] for n in range(n_tasks)]
    for it in range(n_steps):
        for n in range(n_tasks):
            for e in range(2):
                a_sq = a_pad[n][e][:, :CHUNK].astype(BF16)
                if it < n_steps - 1:
                    z = jnp.concatenate([a_pad[n][e], xs[n][e]], axis=1).astype(BF16)
                    rz = _dot(a_sq, z)
                    a_pad[n][e] = rz[:, :LANES]
                    xs[n][e] = xs[n][e] + rz[:, LANES:]
                else:
                    xs[n][e] = xs[n][e] + _dot(a_sq, xs[n][e].astype(BF16))

    ys = [[None] * N_PAIRS for _ in range(nb)]
    for n, (i, g) in enumerate(tasks):
        u_g = jnp.where(head_lanes[0], xs[n][0], xs[n][1])
        uv = jnp.concatenate([u_g.astype(BF16), v_g[n]], axis=0)
        bots = [sc[n][e][CHUNK:].astype(BF16) for e in range(2)]
        ys[i][g] = from_state[n][CHUNK:] + jnp.where(head_lanes[0], _dot(bots[0], uv),
                                                     _dot(bots[1], uv))
        bk_o = jnp.concatenate([tile(b_o, i, g), tile(k_o, i, g)], axis=0).astype(BF16)
        upd = _dot_tn(uv, bk_o)
        p_tot = jnp.exp(totals[i][:, g * LANES:(g + 1) * LANES])
        s_ref[i, g] = state[n] * p_tot + jnp.where(bd_mask, upd, 0.0)

    y = jnp.concatenate([jnp.concatenate(ys[i], axis=1) for i in range(nb)], axis=0)
    if not reverse:
        out_ref[...] = y.reshape(nb, CHUNK, D_RWKV)
        return

    out = yf_ref[...].reshape(rows, D_RWKV) + y
    inv_n = 1.0 / HEAD_DIM
    mean = _dot_split(out, seg) * inv_n
    cen = out - mean
    var = _dot_split(cen * cen, seg) * inv_n
    out = cen * lax.rsqrt(var + GN_EPS) * gng_ref[...] + gnb_ref[...]
    a_o = _sigmoid(a0o_ref[...] + _dot(ad_b, aupo_ref[...]))
    k_sum = k_d + k * (1.0 + (a_o - 1.0) * ka_ref[...])
    bonus = _dot_split(r * k_sum * rk_ref[...], seg)
    out = out + bonus * v
    gd = ps[:, COL_GD:RWKV_COLS]
    gate = _dot(_sigmoid(gd).astype(BF16), gup_ref[...])
    out_ref[...] = (out * gate).reshape(nb, CHUNK, D_RWKV).astype(out_ref.dtype)


def _wkv_call(reverse, p_rw, shared, dir_params, extra):
    b, t, _ = p_rw.shape
    n_chunks = t // CHUNK
    rows8 = t // 8
    per8 = CHUNK // 8
    nb = math.gcd(b, WKV_SEQS)

    def tb_of(c):
        return (n_chunks - 1 - c) if reverse else c

    cur = lambda bi, c: (bi, tb_of(c), 0)
    prev = lambda bi, c: (bi, jnp.maximum(tb_of(c) * per8 - 1, 0), 0)
    nxt = lambda bi, c: (bi, jnp.minimum((tb_of(c) + 1) * per8, rows8 - 1), 0)

    args = [p_rw, p_rw, p_rw] + list(shared[:2]) + list(dir_params) + list(shared[2:])
    in_specs = [pl.BlockSpec((nb, CHUNK, RWKV_COLS), cur),
                pl.BlockSpec((nb, 8, RWKV_COLS), prev),
                pl.BlockSpec((nb, 8, RWKV_COLS), nxt)]
    in_specs += [_full(a.shape) for a in args[3:]]
    if reverse:
        y_f = extra[0]
        args += [y_f] + list(extra[1:])
        in_specs += [pl.BlockSpec((nb, CHUNK, D_RWKV), cur)] + [_full(a.shape) for a in extra[1:]]
        out_dtype = BF16
    else:
        out_dtype = F32
    return pl.pallas_call(
        functools.partial(_wkv_kernel, reverse, n_chunks, nb),
        out_shape=jax.ShapeDtypeStruct((b, t, D_RWKV), out_dtype),
        grid=(b // nb, n_chunks),
        in_specs=in_specs,
        out_specs=pl.BlockSpec((nb, CHUNK, D_RWKV), cur),
        scratch_shapes=[pltpu.VMEM((nb, N_PAIRS, LANES, LANES), F32)],
        compiler_params=_params(("parallel", "arbitrary")),
        name="wkv_bwd" if reverse else "wkv_fwd",
    )(*args)


def _nat_kernel(n_rows, q_ref, k_ref, v_ref, bias_ref, out_ref):
    i = pl.program_id(1)
    rs = jnp.clip(i - WIN_H // 2, 0, n_rows - WIN_H)
    start = pl.multiple_of(rs * GRID_W, GRID_W)
    lane = lax.broadcasted_iota(jnp.int32, (1, LANES), 1)
    head_lanes = (lane < HEAD_DIM, lane >= HEAD_DIM)
    scale = HEAD_DIM ** -0.5
    for g in range(N_PAIRS):
        sl = slice(g * LANES, (g + 1) * LANES)
        kw = k_ref[0, pl.ds(start, WIN_TOKENS), sl]
        vw = v_ref[0, pl.ds(start, WIN_TOKENS), sl]
        qg = q_ref[0, :, sl].astype(F32) * scale
        outs = []
        for e in range(2):
            qe = jnp.where(head_lanes[e], qg, 0.0).astype(BF16)
            s = _dot_nt(qe, kw) + bias_ref[2 * g + e, 0].astype(F32)
            m = jnp.max(s, axis=-1, keepdims=True)
            pr = jnp.exp(s - m)
            l = jnp.sum(pr, axis=-1, keepdims=True)
            outs.append(_dot(pr.astype(BF16), vw) / l)
        out_ref[0, :, sl] = jnp.where(head_lanes[0], outs[0], outs[1]).astype(out_ref.dtype)


def _nat_bias_table(rpb):
    c = np.arange(GRID_W)[:, None]
    cp = np.arange(GRID_W)[None, :]
    cs = np.clip(c - WIN_W // 2, 0, GRID_W - WIN_W)
    valid = (cp >= cs) & (cp < cs + WIN_W)
    dj = np.clip(cp - c + (WIN_W - 1), 0, 2 * WIN_W - 2)
    di = np.arange(WIN_H)[None, :] - np.arange(WIN_H)[:, None] + (WIN_H - 1)
    tab = rpb.astype(F32)[:, di][:, :, :, dj]
    tab = jnp.where(valid[None, None, None], tab, NEG_BIG)
    tab = tab.transpose(0, 1, 3, 2, 4).reshape(N_NAT_HEADS, WIN_H, GRID_W, WIN_TOKENS)
    return tab.astype(BF16)


def _nat_call(p_nat, bias_tab):
    b, t, _ = p_nat.shape
    n_rows = t // GRID_W
    half = WIN_H // 2

    def delta_of(i):
        return i - jnp.clip(i - half, 0, n_rows - WIN_H)

    return pl.pallas_call(
        functools.partial(_nat_kernel, n_rows),
        out_shape=jax.ShapeDtypeStruct((b, t, D_NAT), BF16),
        grid=(b, n_rows),
        in_specs=[pl.BlockSpec((1, GRID_W, D_NAT), lambda bi, i: (bi, i, 0)),
                  pl.BlockSpec((1, t, D_NAT), lambda bi, i: (bi, 0, 1)),
                  pl.BlockSpec((1, t, D_NAT), lambda bi, i: (bi, 0, 2)),
                  pl.BlockSpec((N_NAT_HEADS, 1, GRID_W, WIN_TOKENS),
                               lambda bi, i: (0, delta_of(i), 0, 0))],
        out_specs=pl.BlockSpec((1, GRID_W, D_NAT), lambda bi, i: (bi, i, 0)),
        compiler_params=_params(("parallel", "arbitrary")),
        name="nat",
    )(p_nat, p_nat, p_nat, bias_tab)


def _merge_xattn_kernel(x_ref, yr_ref, yn_ref, gate_ref, kv_ref, wbr_ref, wbn_ref, wout_ref,
                        gx_ref, wq_ref, wo_ref, out_ref):
    gates = gate_ref[0].astype(F32)
    mixed = (_sigmoid(gates[:, :D_MODEL]) * _dot(yr_ref[0], wbr_ref[...])
             + _sigmoid(gates[:, D_MODEL:]) * _dot(yn_ref[0], wbn_ref[...]))
    x1 = x_ref[0] + _dot(mixed.astype(BF16), wout_ref[...])
    q = _dot(_rmsnorm(x1, gx_ref[...]).astype(BF16), wq_ref[...])
    scale = XATTN_HEAD_DIM ** -0.5
    heads = []
    for h in range(N_XATTN_HEADS):
        sl = slice(h * XATTN_HEAD_DIM, (h + 1) * XATTN_HEAD_DIM)
        qh = (q[:, sl] * scale).astype(BF16)
        kh = kv_ref[0, :, sl]
        vh = kv_ref[0, :, D_MODEL + h * XATTN_HEAD_DIM:D_MODEL + (h + 1) * XATTN_HEAD_DIM]
        s = _dot_nt(qh, kh)
        m = jnp.max(s, axis=-1, keepdims=True)
        pr = jnp.exp(s - m)
        l = jnp.sum(pr, axis=-1, keepdims=True)
        heads.append(_dot(pr.astype(BF16), vh) / l)
    o = jnp.concatenate(heads, axis=1).astype(BF16)
    out_ref[0] = x1 + _dot(o, wo_ref[...])


def _merge_xattn_call(x, y_r, y_n, gates, kv, wbr, wbn, wout, gx, wq, wo, tm):
    b, t, d = x.shape
    n_mem = kv.shape[1]
    tile = lambda w: pl.BlockSpec((1, tm, w), lambda bi, i: (bi, i, 0))
    ws = [wbr, wbn, wout, gx.reshape(1, d), wq, wo]
    return pl.pallas_call(
        _merge_xattn_kernel,
        out_shape=jax.ShapeDtypeStruct((b, t, d), F32),
        grid=(b, t // tm),
        in_specs=[tile(d), tile(D_RWKV), tile(D_NAT), tile(GATE_COLS),
                  pl.BlockSpec((1, n_mem, 2 * d), lambda bi, i: (bi, 0, 0))]
        + [_full(w.shape) for w in ws],
        out_specs=tile(d),
        compiler_params=_params(("parallel", "arbitrary")),
        name="merge_xattn",
    )(x, y_r, y_n, gates, kv, *ws)


def _ffn_kernel(final, ff_chunk, x_ref, g_ref, w1_ref, w2_ref, gf_ref, out_ref):
    x = x_ref[...]
    h = _rmsnorm(x, g_ref[...]).astype(BF16)
    acc = x
    for j in range(D_FF // ff_chunk):
        sl = slice(j * ff_chunk, (j + 1) * ff_chunk)
        hf = jnp.maximum(_dot(h, w1_ref[:, sl]), 0.0)
        acc = acc + _dot((hf * hf).astype(BF16), w2_ref[sl, :])
    if final:
        acc = _rmsnorm(acc, gf_ref[...])
    out_ref[...] = acc


def _ffn_call(x2d, g, w1, w2, g_final, final, tm):
    n, d = x2d.shape
    single = pl.Buffered(1)
    return pl.pallas_call(
        functools.partial(_ffn_kernel, final, 1024),
        out_shape=jax.ShapeDtypeStruct((n, d), F32),
        grid=(n // tm,),
        in_specs=[pl.BlockSpec((tm, d), lambda i: (i, 0)), _full((1, d)),
                  pl.BlockSpec(w1.shape, lambda i: (0, 0), pipeline_mode=single),
                  pl.BlockSpec(w2.shape, lambda i: (0, 0), pipeline_mode=single),
                  _full((1, d))],
        out_specs=pl.BlockSpec((tm, d), lambda i: (i, 0)),
        compiler_params=_params(("parallel",)),
        name="ffn",
    )(x2d, g.reshape(1, d), w1, w2, g_final.reshape(1, d))


def _pad_dir(w, d):
    z = jnp.zeros_like(w[d])
    parts = [w[0], z] if d == 0 else [z, w[1]]
    return jnp.concatenate(parts, axis=0).astype(BF16)


def _trunk(x, mem, norm_mix, w_in, mu_prev, mu_next, w0, w_up, a0, a_up, g_up, k_k, k_a, r_k,
           gn_g, gn_b, rpb, w_br_rwkv, w_br_nat, w_out, norm_x, norm_mem, w_xq, w_xkv, w_xo,
           norm_ff, w_ff1, w_ff2, norm_final, tm=256):
    b, t, d = x.shape
    n_mem = mem.shape[1]
    depth = w_in.shape[0]
    head_id = np.arange(D_RWKV) // HEAD_DIM
    seg = jnp.asarray(head_id[:, None] == head_id[None, :], dtype=BF16)
    row = lambda a: a.reshape(1, -1).astype(F32)
    mem2d = mem.reshape(b * n_mem, d)
    for l in range(depth):
        w_in_l = w_in[l].astype(BF16)
        p_rw, p_nat, p_gate = _norm_proj(
            x.reshape(b * t, d), norm_mix[l],
            [w_in_l[:, :RWKV_COLS], w_in_l[:, RWKV_COLS:RWKV_COLS + NAT_COLS],
             w_in_l[:, RWKV_COLS + NAT_COLS:]],
            [F32, BF16, BF16], tm, "in_proj")
        p_rw = p_rw.reshape(b, t, RWKV_COLS)
        p_nat = p_nat.reshape(b, t, NAT_COLS)
        p_gate = p_gate.reshape(b, t, GATE_COLS)

        shared = [row(mu_prev[l]), row(mu_next[l]), row(k_k[l]), row(k_a[l]), seg]
        dir_params = lambda dd: [row(w0[l, dd]), _pad_dir(w_up[l], dd), row(a0[l, dd]),
                                 _pad_dir(a_up[l], dd)]
        y_f = _wkv_call(False, p_rw, shared, dir_params(0), None)
        extra = [y_f, row(a0[l, 0]), _pad_dir(a_up[l], 0), g_up[l].astype(BF16), row(r_k[l]),
                 row(gn_g[l]), row(gn_b[l])]
        y_r = _wkv_call(True, p_rw, shared, dir_params(1), extra)

        y_n = _nat_call(p_nat, _nat_bias_table(rpb[l]))

        (kv,) = _norm_proj(mem2d, norm_mem[l], [w_xkv[l].astype(BF16)], [BF16], tm, "mem_kv")
        kv = kv.reshape(b, n_mem, 2 * d)
        x = _merge_xattn_call(x, y_r, y_n, p_gate, kv, w_br_rwkv[l].astype(BF16),
                              w_br_nat[l].astype(BF16), w_out[l].astype(BF16), norm_x[l],
                              w_xq[l].astype(BF16), w_xo[l].astype(BF16), tm)
        x = _ffn_call(x.reshape(b * t, d), norm_ff[l], w_ff1[l].astype(BF16),
                      w_ff2[l].astype(BF16), norm_final, l == depth - 1, tm).reshape(b, t, d)
    return x


def kernel(x_prompt, x_sample, mem_prompt, mem_sample, norm_mix, w_in, mu_prev, mu_next, w0, w_up, a0, a_up, g_up, k_k, k_a, r_k, gn_g, gn_b, rpb, w_br_rwkv, w_br_nat, w_out, norm_x, norm_mem, w_xq, w_xkv, w_xo, norm_ff, w_ff1, w_ff2, norm_final):
    assert x_prompt.shape[1:] == x_sample.shape[1:] and mem_prompt.shape[1:] == mem_sample.shape[1:]
    nb = x_prompt.shape[0]
    x = jnp.concatenate([x_prompt, x_sample], axis=0)
    mem = jnp.concatenate([mem_prompt, mem_sample], axis=0)
    y = _trunk(x, mem, norm_mix, w_in, mu_prev, mu_next, w0, w_up, a0, a_up, g_up, k_k, k_a,
               r_k, gn_g, gn_b, rpb, w_br_rwkv, w_br_nat, w_out, norm_x, norm_mem, w_xq, w_xkv,
               w_xo, norm_ff, w_ff1, w_ff2, norm_final)
    return (y[:nb], y[nb:])
```

```python
import functools
import math

import numpy as np
import jax
import jax.numpy as jnp
from jax import lax
from jax.experimental import pallas as pl
from jax.experimental.pallas import tpu as pltpu

F32 = jnp.float32
BF16 = jnp.bfloat16

D_MODEL = 1024
GRID_W = 64
HEAD_DIM = 64
D_RWKV = 512
D_NAT = 512
N_NAT_HEADS = D_NAT // HEAD_DIM
LORA_W = 64
LORA_A = 64
LORA_G = 128
WIN_H = 8
WIN_W = 16
N_XATTN_HEADS = 4
XATTN_HEAD_DIM = D_MODEL // N_XATTN_HEADS
D_FF = 4 * D_MODEL
NORM_EPS = 1e-6
GN_EPS = 1e-5 * HEAD_DIM
RWKV_COLS = 3 * D_RWKV + 2 * LORA_W + 2 * LORA_A + LORA_G
NAT_COLS = 3 * D_NAT
GATE_COLS = 2 * D_MODEL
COL_WD = 3 * D_RWKV
COL_AD = COL_WD + 2 * LORA_W
COL_GD = COL_AD + 2 * LORA_A

LANES = 128
N_PAIRS = D_RWKV // LANES
CHUNK = 64
WKV_SEQS = 4
NAT_ROWS = 2
WIN_TOKENS = WIN_H * GRID_W
NEG_BIG = -1e30
VMEM_LIMIT = 56 * 1024 * 1024


def _dot(a, b):
    return jnp.dot(a, b, preferred_element_type=F32)


def _dot_nt(a, b):
    return lax.dot_general(a, b, (((1,), (1,)), ((), ())), preferred_element_type=F32)


def _dot_tn(a, b):
    return lax.dot_general(a, b, (((0,), (0,)), ((), ())), preferred_element_type=F32)


def _dot_split_lhs(a, b):
    hi = b.astype(BF16)
    lo = (b - hi.astype(F32)).astype(BF16)
    return _dot(a, hi) + _dot(a, lo)


def _sigmoid(x):
    return 1.0 / (1.0 + jnp.exp(-x))


def _rmsnorm(x, g):
    return x * lax.rsqrt(jnp.mean(x * x, axis=-1, keepdims=True) + NORM_EPS) * g


def _params(sem):
    return pltpu.CompilerParams(dimension_semantics=sem, vmem_limit_bytes=VMEM_LIMIT)


def _full(shape):
    nd = len(shape)
    return pl.BlockSpec(shape, lambda *_: (0,) * nd)


def _norm_proj_kernel(n_out, x_ref, g_ref, *refs):
    w_refs, o_refs = refs[:n_out], refs[n_out:]
    h = _rmsnorm(x_ref[...], g_ref[...]).astype(BF16)
    for w_ref, o_ref in zip(w_refs, o_refs):
        o_ref[...] = _dot(h, w_ref[...]).astype(o_ref.dtype)


def _norm_proj(x2d, g, ws, out_dtypes, tm, name):
    n, d = x2d.shape
    n_out = len(ws)
    return pl.pallas_call(
        functools.partial(_norm_proj_kernel, n_out),
        out_shape=[jax.ShapeDtypeStruct((n, w.shape[1]), dt) for w, dt in zip(ws, out_dtypes)],
        grid=(n // tm,),
        in_specs=[pl.BlockSpec((tm, d), lambda i: (i, 0)), _full((1, d))]
        + [_full(w.shape) for w in ws],
        out_specs=[pl.BlockSpec((tm, w.shape[1]), lambda i: (i, 0)) for w in ws],
        compiler_params=_params(("parallel",)),
        name=name,
    )(x2d, g.reshape(1, d), *ws)


def _in_proj_kernel(tiles_per_seq, x_ref, xp_ref, xn_ref, g_ref, mup_ref, mun_ref, wrw_ref,
                    wnat_ref, wgate_ref, prw_ref, pnat_ref, pgate_ref):
    i = pl.program_id(0)
    tm = x_ref.shape[0]
    g = g_ref[...]
    h = _rmsnorm(x_ref[...], g)
    hb = h.astype(BF16)
    pnat_ref[...] = _dot(hb, wnat_ref[...]).astype(pnat_ref.dtype)
    pgate_ref[...] = _dot(hb, wgate_ref[...]).astype(pgate_ref.dtype)
    pos = i % tiles_per_seq
    keep_prev = jnp.where(pos == 0, 0.0, 1.0)
    keep_next = jnp.where(pos == tiles_per_seq - 1, 0.0, 1.0)
    h_ext = jnp.concatenate([_rmsnorm(xp_ref[...], g) * keep_prev, h,
                             _rmsnorm(xn_ref[...], g) * keep_next], axis=0)
    p = _dot(h_ext.astype(BF16), wrw_ref[...])
    p_mid = p[8:8 + tm]
    p_prev = pltpu.roll(p, 1, 0)[8:8 + tm]
    p_next = pltpu.roll(p, tm + 15, 0)[8:8 + tm]
    prw_ref[...] = p_mid + mup_ref[...] * (p_prev - p_mid) + mun_ref[...] * (p_next - p_mid)


def _in_proj(x2d, g, mup, mun, w_rw, w_nat, w_gate, seq_len, tm):
    n, d = x2d.shape
    per8 = tm // 8
    ws = [w_rw, w_nat, w_gate]
    return pl.pallas_call(
        functools.partial(_in_proj_kernel, seq_len // tm),
        out_shape=[jax.ShapeDtypeStruct((n, RWKV_COLS), F32),
                   jax.ShapeDtypeStruct((n, NAT_COLS), BF16),
                   jax.ShapeDtypeStruct((n, GATE_COLS), BF16)],
        grid=(n // tm,),
        in_specs=[pl.BlockSpec((tm, d), lambda i: (i, 0)),
                  pl.BlockSpec((8, d), lambda i: (jnp.maximum(i * per8 - 1, 0), 0)),
                  pl.BlockSpec((8, d), lambda i: (jnp.minimum((i + 1) * per8, n // 8 - 1), 0)),
                  _full((1, d)), _full(mup.shape), _full(mun.shape)]
        + [_full(w.shape) for w in ws],
        out_specs=[pl.BlockSpec((tm, w.shape[1]), lambda i: (i, 0)) for w in ws],
        compiler_params=_params(("parallel",)),
        name="in_proj",
    )(x2d, x2d, x2d, g.reshape(1, d), mup, mun, *ws)


def _seg_sum(x, seg):
    w = seg.shape[0]
    return jnp.concatenate([_dot(x[:, j:j + w].astype(BF16), seg) for j in range(0, x.shape[1], w)],
                           axis=1)


def _wkv_kernel(reverse, n_chunks, nb, *refs):
    if reverse:
        (p_ref, w0_ref, wup_ref, a0_ref, aup_ref,
         kk_ref, ka_ref, seg_ref, yf_ref, a0o_ref, aupo_ref, gup_ref, rk_ref, gng_ref, gnb_ref,
         out_ref, s_ref) = refs
    else:
        (p_ref, w0_ref, wup_ref, a0_ref, aup_ref,
         kk_ref, ka_ref, seg_ref, out_ref, s_ref) = refs

    c = pl.program_id(1)
    rows = nb * CHUNK

    @pl.when(c == 0)
    def _():
        s_ref[...] = jnp.zeros_like(s_ref)

    ps = p_ref[...].reshape(rows, RWKV_COLS)
    r = ps[:, 0:D_RWKV]
    k = ps[:, D_RWKV:2 * D_RWKV]
    v = ps[:, 2 * D_RWKV:3 * D_RWKV]
    wd = ps[:, COL_WD:COL_AD]
    ad = ps[:, COL_AD:COL_GD]
    seg = seg_ref[...]

    w_raw = w0_ref[...] + _dot(jnp.tanh(wd).astype(BF16), wup_ref[...])
    logw = (-math.exp(-0.5)) * _sigmoid(w_raw)
    ad_b = ad.astype(BF16)
    a_d = _sigmoid(a0_ref[...] + _dot(ad_b, aup_ref[...]))
    kkv = k * kk_ref[...]
    kkn = kkv * lax.rsqrt(jnp.maximum(_seg_sum(kkv * kkv, seg), 1e-24))
    k_d = k * (1.0 + (a_d - 1.0) * ka_ref[...])
    b_vec = kkn * a_d

    ti = lax.broadcasted_iota(jnp.int32, (CHUNK, CHUNK), 0)
    si = lax.broadcasted_iota(jnp.int32, (CHUNK, CHUNK), 1)
    tri = ((si >= ti) if reverse else (si <= ti)).astype(BF16)
    last = 0 if reverse else CHUNK - 1
    cums, totals = [], []
    for i in range(nb):
        cum_i = _dot_split_lhs(tri, logw[i * CHUNK:(i + 1) * CHUNK])
        cums.append(cum_i)
        totals.append(cum_i[last:last + 1])
    cum = jnp.concatenate(cums, axis=0) if nb > 1 else cums[0]
    total = (jnp.concatenate([jnp.broadcast_to(tt, (CHUNK, D_RWKV)) for tt in totals], axis=0)
             if nb > 1 else totals[0])
    a_t = (-kkn) * jnp.exp(cum - logw)
    r_t = r * jnp.exp(cum)
    e_inv = jnp.exp(-cum)
    b_t = b_vec * e_inv
    k_t = k_d * e_inv
    e_out = jnp.exp(total - cum)
    b_o = b_vec * e_out
    k_o = k_d * e_out

    ri = lax.broadcasted_iota(jnp.int32, (2 * CHUNK, LANES), 0)
    li = lax.broadcasted_iota(jnp.int32, (2 * CHUNK, LANES), 1)
    t_idx = ri % CHUNK
    s_idx = li % CHUNK
    incl = (ri >= CHUNK).astype(jnp.int32)
    sc_mask = (s_idx > t_idx - incl) if reverse else (s_idx < t_idx + incl)
    lane = lax.broadcasted_iota(jnp.int32, (1, LANES), 1)
    head_lanes = (lane < HEAD_DIM, lane >= HEAD_DIM)
    bd_mask = (ri // HEAD_DIM) == (li // HEAD_DIM)
    zeros_cv = jnp.zeros((CHUNK, LANES), BF16)

    tasks = [(i, g) for i in range(nb) for g in range(N_PAIRS)]
    n_tasks = len(tasks)
    n_steps = int(math.log2(CHUNK))

    def tile(a, i, g):
        return a[i * CHUNK:(i + 1) * CHUNK, g * LANES:(g + 1) * LANES]

    lhs_f = [jnp.concatenate([tile(a_t, i, g), tile(r_t, i, g)], axis=0) for i, g in tasks]
    rhs = [jnp.concatenate([tile(b_t, i, g), tile(k_t, i, g)], axis=0).astype(BF16)
           for i, g in tasks]
    v_g = [tile(v, i, g).astype(BF16) for i, g in tasks]
    state = [s_ref[i, g] for i, g in tasks]
    from_state = [_dot_nt(lf.astype(BF16), st.astype(BF16)) for lf, st in zip(lhs_f, state)]
    sc = [[jnp.where(sc_mask, _dot_nt(jnp.where(head_lanes[e], lf, 0.0).astype(BF16), rh), 0.0)
           for e in range(2)] for lf, rh in zip(lhs_f, rhs)]
    zv = [jnp.concatenate([zeros_cv, vg], axis=0) for vg in v_g]
    akv = [[_dot(sc[n][e][:CHUNK].astype(BF16), zv[n]) for e in range(2)] for n in range(n_tasks)]
    x0 = [from_state[n][:CHUNK] + jnp.where(head_lanes[0], akv[n][0], akv[n][1])
          for n in range(n_tasks)]

    a_pad = [[jnp.where(head_lanes[0], sc[n][e][:CHUNK], 0.0) for e in range(2)]
             for n in range(n_tasks)]
    xs = [[x0[n], x0[n]] for n in range(n_tasks)]
    for it in range(n_steps):
        for n in range(n_tasks):
            for e in range(2):
                a_sq = a_pad[n][e][:, :CHUNK].astype(BF16)
                if it < n_steps - 1:
                    z = jnp.concatenate([a_pad[n][e], xs[n][e]], axis=1).astype(BF16)
                    rz = _dot(a_sq, z)
                    a_pad[n][e] = rz[:, :LANES]
                    xs[n][e] = xs[n][e] + rz[:, LANES:]
                else:
                    xs[n][e] = xs[n][e] + _dot(a_sq, xs[n][e].astype(BF16))

    ys = [[None] * N_PAIRS for _ in range(nb)]
    for n, (i, g) in enumerate(tasks):
        u_g = jnp.where(head_lanes[0], xs[n][0], xs[n][1])
        uv = jnp.concatenate([u_g.astype(BF16), v_g[n]], axis=0)
        bots = [sc[n][e][CHUNK:].astype(BF16) for e in range(2)]
        ys[i][g] = from_state[n][CHUNK:] + jnp.where(head_lanes[0], _dot(bots[0], uv),
                                                     _dot(bots[1], uv))
        bk_o = jnp.concatenate([tile(b_o, i, g), tile(k_o, i, g)], axis=0).astype(BF16)
        upd = _dot_tn(uv, bk_o)
        p_tot = jnp.exp(totals[i][:, g * LANES:(g + 1) * LANES])
        s_ref[i, g] = state[n] * p_tot + jnp.where(bd_mask, upd, 0.0)

    y = jnp.concatenate([jnp.concatenate(ys[i], axis=1) for i in range(nb)], axis=0)
    if not reverse:
        out_ref[...] = y.reshape(nb, CHUNK, D_RWKV)
        return

    out = yf_ref[...].reshape(rows, D_RWKV) + y
    inv_n = 1.0 / HEAD_DIM
    mean = _seg_sum(out, seg) * inv_n
    cen = out - mean
    var = _seg_sum(cen * cen, seg) * inv_n
    out = cen * lax.rsqrt(var + GN_EPS) * gng_ref[...] + gnb_ref[...]
    a_o = _sigmoid(a0o_ref[...] + _dot(ad_b, aupo_ref[...]))
    k_sum = k_d + k * (1.0 + (a_o - 1.0) * ka_ref[...])
    bonus = _seg_sum(r * k_sum * rk_ref[...], seg)
    out = out + bonus * v
    gd = ps[:, COL_GD:RWKV_COLS]
    gate = _dot(_sigmoid(gd).astype(BF16), gup_ref[...])
    out_ref[...] = (out * gate).reshape(nb, CHUNK, D_RWKV).astype(out_ref.dtype)


def _wkv_call(reverse, p_rw, shared, dir_params, extra):
    b, t, _ = p_rw.shape
    n_chunks = t // CHUNK
    nb = math.gcd(b, WKV_SEQS)
    cur = lambda bi, c: (bi, (n_chunks - 1 - c) if reverse else c, 0)
    args = [p_rw] + list(dir_params) + list(shared)
    in_specs = [pl.BlockSpec((nb, CHUNK, RWKV_COLS), cur)] + [_full(a.shape) for a in args[1:]]
    if reverse:
        y_f = extra[0]
        args += [y_f] + list(extra[1:])
        in_specs += [pl.BlockSpec((nb, CHUNK, D_RWKV), cur)] + [_full(a.shape) for a in extra[1:]]
        out_dtype = BF16
    else:
        out_dtype = F32
    return pl.pallas_call(
        functools.partial(_wkv_kernel, reverse, n_chunks, nb),
        out_shape=jax.ShapeDtypeStruct((b, t, D_RWKV), out_dtype),
        grid=(b // nb, n_chunks),
        in_specs=in_specs,
        out_specs=pl.BlockSpec((nb, CHUNK, D_RWKV), cur),
        scratch_shapes=[pltpu.VMEM((nb, N_PAIRS, LANES, LANES), F32)],
        compiler_params=_params(("parallel", "arbitrary")),
        name="wkv_bwd" if reverse else "wkv_fwd",
    )(*args)


def _nat_kernel(n_rows, rb, q_ref, k_ref, v_ref, bias_ref, out_ref):
    i0 = pl.program_id(1) * rb
    lane = lax.broadcasted_iota(jnp.int32, (1, LANES), 1)
    head_lanes = (lane < HEAD_DIM, lane >= HEAD_DIM)
    scale = HEAD_DIM ** -0.5
    starts, deltas = [], []
    for j in range(rb):
        rs = jnp.clip(i0 + j - WIN_H // 2, 0, n_rows - WIN_H)
        starts.append(pl.multiple_of(rs * GRID_W, GRID_W))
        deltas.append(i0 + j - rs)
    tasks = [(j, g) for j in range(rb) for g in range(N_PAIRS)]
    lanes = lambda g: slice(g * LANES, (g + 1) * LANES)
    kw = [k_ref[0, pl.ds(starts[j], WIN_TOKENS), lanes(g)] for j, g in tasks]
    vw = [v_ref[0, pl.ds(starts[j], WIN_TOKENS), lanes(g)] for j, g in tasks]
    qg = [q_ref[0, j * GRID_W:(j + 1) * GRID_W, lanes(g)].astype(F32) * scale for j, g in tasks]
    s = [[_dot_nt(jnp.where(head_lanes[e], qg[n], 0.0).astype(BF16), kw[n])
          + bias_ref[2 * g + e, deltas[j]].astype(F32) for e in range(2)]
         for n, (j, g) in enumerate(tasks)]
    pr = [[jnp.exp(s[n][e] - jnp.max(s[n][e], axis=-1, keepdims=True)) for e in range(2)]
          for n in range(len(tasks))]
    o = [[_dot(pr[n][e].astype(BF16), vw[n]) / jnp.sum(pr[n][e], axis=-1, keepdims=True)
          for e in range(2)] for n in range(len(tasks))]
    for n, (j, g) in enumerate(tasks):
        out_ref[0, j * GRID_W:(j + 1) * GRID_W, lanes(g)] = jnp.where(
            head_lanes[0], o[n][0], o[n][1]).astype(out_ref.dtype)


def _nat_bias_table(rpb):
    c = np.arange(GRID_W)[:, None]
    cp = np.arange(GRID_W)[None, :]
    cs = np.clip(c - WIN_W // 2, 0, GRID_W - WIN_W)
    valid = (cp >= cs) & (cp < cs + WIN_W)
    dj = np.clip(cp - c + (WIN_W - 1), 0, 2 * WIN_W - 2)
    di = np.arange(WIN_H)[None, :] - np.arange(WIN_H)[:, None] + (WIN_H - 1)
    tab = rpb.astype(F32)[:, di][:, :, :, dj]
    tab = jnp.where(valid[None, None, None], tab, NEG_BIG)
    tab = tab.transpose(0, 1, 3, 2, 4).reshape(N_NAT_HEADS, WIN_H, GRID_W, WIN_TOKENS)
    return tab.astype(BF16)


def _nat_call(p_nat, bias_tab):
    b, t, _ = p_nat.shape
    n_rows = t // GRID_W
    rb = math.gcd(n_rows, NAT_ROWS)
    return pl.pallas_call(
        functools.partial(_nat_kernel, n_rows, rb),
        out_shape=jax.ShapeDtypeStruct((b, t, D_NAT), BF16),
        grid=(b, n_rows // rb),
        in_specs=[pl.BlockSpec((1, rb * GRID_W, D_NAT), lambda bi, i: (bi, i, 0)),
                  pl.BlockSpec((1, t, D_NAT), lambda bi, i: (bi, 0, 1)),
                  pl.BlockSpec((1, t, D_NAT), lambda bi, i: (bi, 0, 2)),
                  _full(bias_tab.shape)],
        out_specs=pl.BlockSpec((1, rb * GRID_W, D_NAT), lambda bi, i: (bi, i, 0)),
        compiler_params=_params(("parallel", "arbitrary")),
        name="nat",
    )(p_nat, p_nat, p_nat, bias_tab)


def _merge_xattn_kernel(x_ref, yr_ref, yn_ref, gate_ref, kv_ref, wbr_ref, wbn_ref, wout_ref,
                        gx_ref, wq_ref, wo_ref, out_ref):
    gates = gate_ref[0].astype(F32)
    mixed = (_sigmoid(gates[:, :D_MODEL]) * _dot(yr_ref[0], wbr_ref[...])
             + _sigmoid(gates[:, D_MODEL:]) * _dot(yn_ref[0], wbn_ref[...]))
    x1 = x_ref[0] + _dot(mixed.astype(BF16), wout_ref[...])
    q = _dot(_rmsnorm(x1, gx_ref[...]).astype(BF16), wq_ref[...])
    scale = XATTN_HEAD_DIM ** -0.5
    heads = []
    for h in range(N_XATTN_HEADS):
        sl = slice(h * XATTN_HEAD_DIM, (h + 1) * XATTN_HEAD_DIM)
        qh = (q[:, sl] * scale).astype(BF16)
        kh = kv_ref[0, :, sl]
        vh = kv_ref[0, :, D_MODEL + h * XATTN_HEAD_DIM:D_MODEL + (h + 1) * XATTN_HEAD_DIM]
        s = _dot_nt(qh, kh)
        m = jnp.max(s, axis=-1, keepdims=True)
        pr = jnp.exp(s - m)
        l = jnp.sum(pr, axis=-1, keepdims=True)
        heads.append(_dot(pr.astype(BF16), vh) / l)
    o = jnp.concatenate(heads, axis=1).astype(BF16)
    out_ref[0] = x1 + _dot(o, wo_ref[...])


def _merge_xattn_call(x, y_r, y_n, gates, kv, wbr, wbn, wout, gx, wq, wo, tm):
    b, t, d = x.shape
    n_mem = kv.shape[1]
    tile = lambda w: pl.BlockSpec((1, tm, w), lambda bi, i: (bi, i, 0))
    ws = [wbr, wbn, wout, gx.reshape(1, d), wq, wo]
    return pl.pallas_call(
        _merge_xattn_kernel,
        out_shape=jax.ShapeDtypeStruct((b, t, d), F32),
        grid=(b, t // tm),
        in_specs=[tile(d), tile(D_RWKV), tile(D_NAT), tile(GATE_COLS),
                  pl.BlockSpec((1, n_mem, 2 * d), lambda bi, i: (bi, 0, 0))]
        + [_full(w.shape) for w in ws],
        out_specs=tile(d),
        compiler_params=_params(("parallel", "arbitrary")),
        name="merge_xattn",
    )(x, y_r, y_n, gates, kv, *ws)


def _ffn_kernel(final, ff_chunk, x_ref, g_ref, w1_ref, w2_ref, gf_ref, out_ref):
    x = x_ref[...]
    h = _rmsnorm(x, g_ref[...]).astype(BF16)
    acc = x
    for j in range(D_FF // ff_chunk):
        sl = slice(j * ff_chunk, (j + 1) * ff_chunk)
        hf = jnp.maximum(_dot(h, w1_ref[:, sl]), 0.0)
        acc = acc + _dot((hf * hf).astype(BF16), w2_ref[sl, :])
    if final:
        acc = _rmsnorm(acc, gf_ref[...])
    out_ref[...] = acc


def _ffn_call(x2d, g, w1, w2, g_final, final, tm):
    n, d = x2d.shape
    single = pl.Buffered(1)
    return pl.pallas_call(
        functools.partial(_ffn_kernel, final, 1024),
        out_shape=jax.ShapeDtypeStruct((n, d), F32),
        grid=(n // tm,),
        in_specs=[pl.BlockSpec((tm, d), lambda i: (i, 0)), _full((1, d)),
                  pl.BlockSpec(w1.shape, lambda i: (0, 0), pipeline_mode=single),
                  pl.BlockSpec(w2.shape, lambda i: (0, 0), pipeline_mode=single),
                  _full((1, d))],
        out_specs=pl.BlockSpec((tm, d), lambda i: (i, 0)),
        compiler_params=_params(("parallel",)),
        name="ffn",
    )(x2d, g.reshape(1, d), w1, w2, g_final.reshape(1, d))


def _pad_dir(w, d):
    z = jnp.zeros_like(w[d])
    parts = [w[0], z] if d == 0 else [z, w[1]]
    return jnp.concatenate(parts, axis=0).astype(BF16)


def _trunk(x, mem, norm_mix, w_in, mu_prev, mu_next, w0, w_up, a0, a_up, g_up, k_k, k_a, r_k,
           gn_g, gn_b, rpb, w_br_rwkv, w_br_nat, w_out, norm_x, norm_mem, w_xq, w_xkv, w_xo,
           norm_ff, w_ff1, w_ff2, norm_final, tm=256):
    b, t, d = x.shape
    n_mem = mem.shape[1]
    depth = w_in.shape[0]
    head_id = np.arange(2 * LANES) // HEAD_DIM
    seg = jnp.asarray(head_id[:, None] == head_id[None, :], dtype=BF16)
    row = lambda a: a.reshape(1, -1).astype(F32)
    mem2d = mem.reshape(b * n_mem, d)
    for l in range(depth):
        w_in_l = w_in[l].astype(BF16)
        p_rw, p_nat, p_gate = _in_proj(
            x.reshape(b * t, d), norm_mix[l], row(mu_prev[l]), row(mu_next[l]),
            w_in_l[:, :RWKV_COLS], w_in_l[:, RWKV_COLS:RWKV_COLS + NAT_COLS],
            w_in_l[:, RWKV_COLS + NAT_COLS:], t, tm)
        p_rw = p_rw.reshape(b, t, RWKV_COLS)
        p_nat = p_nat.reshape(b, t, NAT_COLS)
        p_gate = p_gate.reshape(b, t, GATE_COLS)

        shared = [row(k_k[l]), row(k_a[l]), seg]
        dir_params = lambda dd: [row(w0[l, dd]), _pad_dir(w_up[l], dd), row(a0[l, dd]),
                                 _pad_dir(a_up[l], dd)]
        y_f = _wkv_call(False, p_rw, shared, dir_params(0), None)
        extra = [y_f, row(a0[l, 0]), _pad_dir(a_up[l], 0), g_up[l].astype(BF16), row(r_k[l]),
                 row(gn_g[l]), row(gn_b[l])]
        y_r = _wkv_call(True, p_rw, shared, dir_params(1), extra)

        y_n = _nat_call(p_nat, _nat_bias_table(rpb[l]))

        (kv,) = _norm_proj(mem2d, norm_mem[l], [w_xkv[l].astype(BF16)], [BF16], tm, "mem_kv")
        kv = kv.reshape(b, n_mem, 2 * d)
        x = _merge_xattn_call(x, y_r, y_n, p_gate, kv, w_br_rwkv[l].astype(BF16),
                              w_br_nat[l].astype(BF16), w_out[l].astype(BF16), norm_x[l],
                              w_xq[l].astype(BF16), w_xo[l].astype(BF16), tm)
        x = _ffn_call(x.reshape(b * t, d), norm_ff[l], w_ff1[l].astype(BF16),
                      w_ff2[l].astype(BF16), norm_final, l == depth - 1, tm).reshape(b, t, d)
    return x


def kernel(x_prompt, x_sample, mem_prompt, mem_sample, norm_mix, w_in, mu_prev, mu_next, w0, w_up, a0, a_up, g_up, k_k, k_a, r_k, gn_g, gn_b, rpb, w_br_rwkv, w_br_nat, w_out, norm_x, norm_mem, w_xq, w_xkv, w_xo, norm_ff, w_ff1, w_ff2, norm_final):
    assert x_prompt.shape[1:] == x_sample.shape[1:] and mem_prompt.shape[1:] == mem_sample.shape[1:]
    nb = x_prompt.shape[0]
    x = jnp.concatenate([x_prompt, x_sample], axis=0)
    mem = jnp.concatenate([mem_prompt, mem_sample], axis=0)
    y = _trunk(x, mem, norm_mix, w_in, mu_prev, mu_next, w0, w_up, a0, a_up, g_up, k_k, k_a,
               r_k, gn_g, gn_b, rpb, w_br_rwkv, w_br_nat, w_out, norm_x, norm_mem, w_xq, w_xkv,
               w_xo, norm_ff, w_ff1, w_ff2, norm_final)
    return (y[:nb], y[nb:])
```

```python
import functools
import math

import numpy as np
import jax
import jax.numpy as jnp
from jax import lax
from jax.experimental import pallas as pl
from jax.experimental.pallas import tpu as pltpu

F32 = jnp.float32
BF16 = jnp.bfloat16

D_MODEL = 1024
GRID_W = 64
HEAD_DIM = 64
D_RWKV = 512
D_NAT = 512
N_NAT_HEADS = D_NAT // HEAD_DIM
LORA_W = 64
LORA_A = 64
LORA_G = 128
WIN_H = 8
WIN_W = 16
N_XATTN_HEADS = 4
XATTN_HEAD_DIM = D_MODEL // N_XATTN_HEADS
D_FF = 4 * D_MODEL
NORM_EPS = 1e-6
GN_EPS = 1e-5 * HEAD_DIM
RWKV_COLS = 3 * D_RWKV + 2 * LORA_W + 2 * LORA_A + LORA_G
NAT_COLS = 3 * D_NAT
GATE_COLS = 2 * D_MODEL
COL_WD = 3 * D_RWKV
COL_AD = COL_WD + 2 * LORA_W
COL_GD = COL_AD + 2 * LORA_A

LANES = 128
N_PAIRS = D_RWKV // LANES
CHUNK = 64
WKV_SEQS = 4
NAT_ROWS = 2
WIN_TOKENS = WIN_H * GRID_W
NEG_BIG = -1e30
VMEM_LIMIT = 56 * 1024 * 1024


def _dot(a, b):
    return jnp.dot(a, b, preferred_element_type=F32)


def _dot_nt(a, b):
    return lax.dot_general(a, b, (((1,), (1,)), ((), ())), preferred_element_type=F32)


def _dot_tn(a, b):
    return lax.dot_general(a, b, (((0,), (0,)), ((), ())), preferred_element_type=F32)


def _dot_split_lhs(a, b):
    hi = b.astype(BF16)
    lo = (b - hi.astype(F32)).astype(BF16)
    return _dot(a, hi) + _dot(a, lo)


def _sigmoid(x):
    return 1.0 / (1.0 + jnp.exp(-x))


def _rmsnorm(x, g):
    return x * lax.rsqrt(jnp.mean(x * x, axis=-1, keepdims=True) + NORM_EPS) * g


def _params(sem):
    return pltpu.CompilerParams(dimension_semantics=sem, vmem_limit_bytes=VMEM_LIMIT)


def _full(shape):
    nd = len(shape)
    return pl.BlockSpec(shape, lambda *_: (0,) * nd)


def _norm_proj_kernel(n_out, x_ref, g_ref, *refs):
    w_refs, o_refs = refs[:n_out], refs[n_out:]
    h = _rmsnorm(x_ref[...], g_ref[...]).astype(BF16)
    for w_ref, o_ref in zip(w_refs, o_refs):
        o_ref[...] = _dot(h, w_ref[...]).astype(o_ref.dtype)


def _norm_proj(x2d, g, ws, out_dtypes, tm, name):
    n, d = x2d.shape
    n_out = len(ws)
    return pl.pallas_call(
        functools.partial(_norm_proj_kernel, n_out),
        out_shape=[jax.ShapeDtypeStruct((n, w.shape[1]), dt) for w, dt in zip(ws, out_dtypes)],
        grid=(n // tm,),
        in_specs=[pl.BlockSpec((tm, d), lambda i: (i, 0)), _full((1, d))]
        + [_full(w.shape) for w in ws],
        out_specs=[pl.BlockSpec((tm, w.shape[1]), lambda i: (i, 0)) for w in ws],
        compiler_params=_params(("parallel",)),
        name=name,
    )(x2d, g.reshape(1, d), *ws)


def _tile_starts(row_counts, tm):
    return tuple(int(s) // tm for s in np.cumsum([0] + list(row_counts[:-1])))


def _in_proj_kernel(tiles_per_seq, tile_starts, *refs):
    n_src = len(tile_starts)
    src_refs = refs[:3 * n_src]
    (g_ref, mup_ref, mun_ref, wrw_ref, wnat_ref, wgate_ref,
     prw_ref, pnat_ref, pgate_ref) = refs[3 * n_src:]
    i = pl.program_id(0)

    def pick(k):
        val = src_refs[k][...]
        for s in range(1, n_src):
            val = jnp.where(i >= tile_starts[s], src_refs[3 * s + k][...], val)
        return val

    x = pick(0)
    tm = x.shape[0]
    g = g_ref[...]
    h = _rmsnorm(x, g)
    hb = h.astype(BF16)
    pnat_ref[...] = _dot(hb, wnat_ref[...]).astype(pnat_ref.dtype)
    pgate_ref[...] = _dot(hb, wgate_ref[...]).astype(pgate_ref.dtype)
    pos = i % tiles_per_seq
    keep_prev = jnp.where(pos == 0, 0.0, 1.0)
    keep_next = jnp.where(pos == tiles_per_seq - 1, 0.0, 1.0)
    h_ext = jnp.concatenate([_rmsnorm(pick(1), g) * keep_prev, h,
                             _rmsnorm(pick(2), g) * keep_next], axis=0)
    p = _dot(h_ext.astype(BF16), wrw_ref[...])
    p_mid = p[8:8 + tm]
    p_prev = pltpu.roll(p, 1, 0)[8:8 + tm]
    p_next = pltpu.roll(p, tm + 15, 0)[8:8 + tm]
    prw_ref[...] = p_mid + mup_ref[...] * (p_prev - p_mid) + mun_ref[...] * (p_next - p_mid)


def _in_proj(xs, g, mup, mun, w_rw, w_nat, w_gate, seq_len, tm):
    d = xs[0].shape[1]
    n = sum(x.shape[0] for x in xs)
    per8 = tm // 8
    ws = [w_rw, w_nat, w_gate]
    starts = _tile_starts([x.shape[0] for x in xs], tm)
    src_specs, src_args = [], []
    for x, st in zip(xs, starts):
        nt = x.shape[0] // tm
        local = lambda i, st=st, nt=nt: jnp.clip(i - st, 0, nt - 1)
        last8 = x.shape[0] // 8 - 1
        src_specs += [
            pl.BlockSpec((tm, d), lambda i, local=local: (local(i), 0)),
            pl.BlockSpec((8, d), lambda i, local=local: (jnp.maximum(local(i) * per8 - 1, 0), 0)),
            pl.BlockSpec((8, d), lambda i, local=local, last8=last8:
                         (jnp.minimum((local(i) + 1) * per8, last8), 0))]
        src_args += [x, x, x]
    return pl.pallas_call(
        functools.partial(_in_proj_kernel, seq_len // tm, starts),
        out_shape=[jax.ShapeDtypeStruct((n, RWKV_COLS), F32),
                   jax.ShapeDtypeStruct((n, NAT_COLS), BF16),
                   jax.ShapeDtypeStruct((n, GATE_COLS), BF16)],
        grid=(n // tm,),
        in_specs=src_specs + [_full((1, d)), _full(mup.shape), _full(mun.shape)]
        + [_full(w.shape) for w in ws],
        out_specs=[pl.BlockSpec((tm, w.shape[1]), lambda i: (i, 0)) for w in ws],
        compiler_params=_params(("parallel",)),
        name="in_proj",
    )(*src_args, g.reshape(1, d), mup, mun, *ws)


def _seg_sum(x, seg):
    w = seg.shape[0]
    return jnp.concatenate([_dot(x[:, j:j + w].astype(BF16), seg) for j in range(0, x.shape[1], w)],
                           axis=1)


def _wkv_kernel(reverse, n_chunks, nb, *refs):
    if reverse:
        (p_ref, w0_ref, wup_ref, a0_ref, aup_ref,
         kk_ref, ka_ref, seg_ref, yf_ref, a0o_ref, aupo_ref, gup_ref, rk_ref, gng_ref, gnb_ref,
         out_ref, s_ref) = refs
    else:
        (p_ref, w0_ref, wup_ref, a0_ref, aup_ref,
         kk_ref, ka_ref, seg_ref, out_ref, s_ref) = refs

    c = pl.program_id(1)
    rows = nb * CHUNK

    @pl.when(c == 0)
    def _():
        s_ref[...] = jnp.zeros_like(s_ref)

    ps = p_ref[...].reshape(rows, RWKV_COLS)
    r = ps[:, 0:D_RWKV]
    k = ps[:, D_RWKV:2 * D_RWKV]
    v = ps[:, 2 * D_RWKV:3 * D_RWKV]
    wd = ps[:, COL_WD:COL_AD]
    ad = ps[:, COL_AD:COL_GD]
    seg = seg_ref[...]

    w_raw = w0_ref[...] + _dot(jnp.tanh(wd).astype(BF16), wup_ref[...])
    logw = (-math.exp(-0.5)) * _sigmoid(w_raw)
    ad_b = ad.astype(BF16)
    a_d = _sigmoid(a0_ref[...] + _dot(ad_b, aup_ref[...]))
    kkv = k * kk_ref[...]
    kkn = kkv * lax.rsqrt(jnp.maximum(_seg_sum(kkv * kkv, seg), 1e-24))
    k_d = k * (1.0 + (a_d - 1.0) * ka_ref[...])
    b_vec = kkn * a_d

    ti = lax.broadcasted_iota(jnp.int32, (CHUNK, CHUNK), 0)
    si = lax.broadcasted_iota(jnp.int32, (CHUNK, CHUNK), 1)
    tri = ((si >= ti) if reverse else (si <= ti)).astype(BF16)
    last = 0 if reverse else CHUNK - 1
    cums, totals = [], []
    for i in range(nb):
        cum_i = _dot_split_lhs(tri, logw[i * CHUNK:(i + 1) * CHUNK])
        cums.append(cum_i)
        totals.append(cum_i[last:last + 1])
    cum = jnp.concatenate(cums, axis=0) if nb > 1 else cums[0]
    total = (jnp.concatenate([jnp.broadcast_to(tt, (CHUNK, D_RWKV)) for tt in totals], axis=0)
             if nb > 1 else totals[0])
    a_t = (-kkn) * jnp.exp(cum - logw)
    r_t = r * jnp.exp(cum)
    e_inv = jnp.exp(-cum)
    b_t = b_vec * e_inv
    k_t = k_d * e_inv
    e_out = jnp.exp(total - cum)
    b_o = b_vec * e_out
    k_o = k_d * e_out

    ri = lax.broadcasted_iota(jnp.int32, (2 * CHUNK, LANES), 0)
    li = lax.broadcasted_iota(jnp.int32, (2 * CHUNK, LANES), 1)
    ri2 = lax.broadcasted_iota(jnp.int32, (4 * CHUNK, LANES), 0)
    li2 = lax.broadcasted_iota(jnp.int32, (4 * CHUNK, LANES), 1)
    t_idx = ri2 % CHUNK
    s_idx = li2 % CHUNK
    incl = (ri2 // CHUNK) % 2
    sc_mask2 = (s_idx > t_idx - incl) if reverse else (s_idx < t_idx + incl)
    lane = lax.broadcasted_iota(jnp.int32, (1, LANES), 1)
    head_lanes = (lane < HEAD_DIM, lane >= HEAD_DIM)
    bd_mask = (ri // HEAD_DIM) == (li // HEAD_DIM)
    zeros_cv = jnp.zeros((CHUNK, LANES), BF16)

    tasks = [(i, g) for i in range(nb) for g in range(N_PAIRS)]
    n_tasks = len(tasks)
    n_steps = int(math.log2(CHUNK))

    def tile(a, i, g):
        return a[i * CHUNK:(i + 1) * CHUNK, g * LANES:(g + 1) * LANES]

    lhs_f = [jnp.concatenate([tile(a_t, i, g), tile(r_t, i, g)], axis=0) for i, g in tasks]
    rhs = [jnp.concatenate([tile(b_t, i, g), tile(k_t, i, g)], axis=0).astype(BF16)
           for i, g in tasks]
    v_g = [tile(v, i, g).astype(BF16) for i, g in tasks]
    state = [s_ref[i, g] for i, g in tasks]
    from_state = [_dot_nt(lf.astype(BF16), st.astype(BF16)) for lf, st in zip(lhs_f, state)]
    sc2 = [jnp.where(sc_mask2, _dot_nt(jnp.concatenate(
        [jnp.where(head_lanes[e], lf, 0.0) for e in range(2)], axis=0).astype(BF16), rh), 0.0)
           for lf, rh in zip(lhs_f, rhs)]
    top = [[s2[2 * CHUNK * e:2 * CHUNK * e + CHUNK] for e in range(2)] for s2 in sc2]
    bot = [[s2[2 * CHUNK * e + CHUNK:2 * CHUNK * (e + 1)] for e in range(2)] for s2 in sc2]
    zv = [jnp.concatenate([zeros_cv, vg], axis=0) for vg in v_g]
    akv = [_dot(jnp.concatenate(top[n], axis=0).astype(BF16), zv[n]) for n in range(n_tasks)]
    x0 = [from_state[n][:CHUNK] + jnp.where(head_lanes[0], akv[n][:CHUNK], akv[n][CHUNK:])
          for n in range(n_tasks)]

    a_pad = [[jnp.where(head_lanes[0], top[n][e], 0.0) for e in range(2)]
             for n in range(n_tasks)]
    xs = [[x0[n], x0[n]] for n in range(n_tasks)]
    for it in range(n_steps):
        for n in range(n_tasks):
            for e in range(2):
                a_sq = a_pad[n][e][:, :CHUNK].astype(BF16)
                if it < n_steps - 1:
                    z = jnp.concatenate([a_pad[n][e], xs[n][e]], axis=1).astype(BF16)
                    rz = _dot(a_sq, z)
                    a_pad[n][e] = rz[:, :LANES]
                    xs[n][e] = xs[n][e] + rz[:, LANES:]
                else:
                    xs[n][e] = xs[n][e] + _dot(a_sq, xs[n][e].astype(BF16))

    ys = [[None] * N_PAIRS for _ in range(nb)]
    for n, (i, g) in enumerate(tasks):
        u_g = jnp.where(head_lanes[0], xs[n][0], xs[n][1])
        uv = jnp.concatenate([u_g.astype(BF16), v_g[n]], axis=0)
        rbk = _dot(jnp.concatenate(bot[n], axis=0).astype(BF16), uv)
        ys[i][g] = from_state[n][CHUNK:] + jnp.where(head_lanes[0], rbk[:CHUNK], rbk[CHUNK:])
        bk_o = jnp.concatenate([tile(b_o, i, g), tile(k_o, i, g)], axis=0).astype(BF16)
        upd = _dot_tn(uv, bk_o)
        p_tot = jnp.exp(totals[i][:, g * LANES:(g + 1) * LANES])
        s_ref[i, g] = state[n] * p_tot + jnp.where(bd_mask, upd, 0.0)

    y = jnp.concatenate([jnp.concatenate(ys[i], axis=1) for i in range(nb)], axis=0)
    if not reverse:
        out_ref[...] = y.reshape(nb, CHUNK, D_RWKV)
        return

    out = yf_ref[...].reshape(rows, D_RWKV) + y
    inv_n = 1.0 / HEAD_DIM
    mean = _seg_sum(out, seg) * inv_n
    cen = out - mean
    var = _seg_sum(cen * cen, seg) * inv_n
    out = cen * lax.rsqrt(var + GN_EPS) * gng_ref[...] + gnb_ref[...]
    a_o = _sigmoid(a0o_ref[...] + _dot(ad_b, aupo_ref[...]))
    k_sum = k_d + k * (1.0 + (a_o - 1.0) * ka_ref[...])
    bonus = _seg_sum(r * k_sum * rk_ref[...], seg)
    out = out + bonus * v
    gd = ps[:, COL_GD:RWKV_COLS]
    gate = _dot(_sigmoid(gd).astype(BF16), gup_ref[...])
    out_ref[...] = (out * gate).reshape(nb, CHUNK, D_RWKV).astype(out_ref.dtype)


def _wkv_call(reverse, p_rw, shared, dir_params, extra):
    b, t, _ = p_rw.shape
    n_chunks = t // CHUNK
    nb = math.gcd(b, WKV_SEQS)
    cur = lambda bi, c: (bi, (n_chunks - 1 - c) if reverse else c, 0)
    args = [p_rw] + list(dir_params) + list(shared)
    in_specs = [pl.BlockSpec((nb, CHUNK, RWKV_COLS), cur)] + [_full(a.shape) for a in args[1:]]
    if reverse:
        y_f = extra[0]
        args += [y_f] + list(extra[1:])
        in_specs += [pl.BlockSpec((nb, CHUNK, D_RWKV), cur)] + [_full(a.shape) for a in extra[1:]]
        out_dtype = BF16
    else:
        out_dtype = F32
    return pl.pallas_call(
        functools.partial(_wkv_kernel, reverse, n_chunks, nb),
        out_shape=jax.ShapeDtypeStruct((b, t, D_RWKV), out_dtype),
        grid=(b // nb, n_chunks),
        in_specs=in_specs,
        out_specs=pl.BlockSpec((nb, CHUNK, D_RWKV), cur),
        scratch_shapes=[pltpu.VMEM((nb, N_PAIRS, LANES, LANES), F32)],
        compiler_params=_params(("parallel", "arbitrary")),
        name="wkv_bwd" if reverse else "wkv_fwd",
    )(*args)


def _nat_kernel(n_rows, rb, q_ref, k_ref, v_ref, bias_ref, out_ref):
    i0 = pl.program_id(1) * rb
    lane = lax.broadcasted_iota(jnp.int32, (1, LANES), 1)
    head_lanes = (lane < HEAD_DIM, lane >= HEAD_DIM)
    scale = HEAD_DIM ** -0.5
    starts, deltas = [], []
    for j in range(rb):
        rs = jnp.clip(i0 + j - WIN_H // 2, 0, n_rows - WIN_H)
        starts.append(pl.multiple_of(rs * GRID_W, GRID_W))
        deltas.append(i0 + j - rs)
    tasks = [(j, g) for j in range(rb) for g in range(N_PAIRS)]
    lanes = lambda g: slice(g * LANES, (g + 1) * LANES)
    kw = [k_ref[0, pl.ds(starts[j], WIN_TOKENS), lanes(g)] for j, g in tasks]
    vw = [v_ref[0, pl.ds(starts[j], WIN_TOKENS), lanes(g)] for j, g in tasks]
    qg = [q_ref[0, j * GRID_W:(j + 1) * GRID_W, lanes(g)].astype(F32) * scale for j, g in tasks]
    s = [[_dot_nt(jnp.where(head_lanes[e], qg[n], 0.0).astype(BF16), kw[n])
          + bias_ref[2 * g + e, deltas[j]].astype(F32) for e in range(2)]
         for n, (j, g) in enumerate(tasks)]
    pr = [[jnp.exp(s[n][e] - jnp.max(s[n][e], axis=-1, keepdims=True)) for e in range(2)]
          for n in range(len(tasks))]
    o = [[_dot(pr[n][e].astype(BF16), vw[n]) / jnp.sum(pr[n][e], axis=-1, keepdims=True)
          for e in range(2)] for n in range(len(tasks))]
    for n, (j, g) in enumerate(tasks):
        out_ref[0, j * GRID_W:(j + 1) * GRID_W, lanes(g)] = jnp.where(
            head_lanes[0], o[n][0], o[n][1]).astype(out_ref.dtype)


def _nat_bias_table(rpb):
    c = np.arange(GRID_W)[:, None]
    cp = np.arange(GRID_W)[None, :]
    cs = np.clip(c - WIN_W // 2, 0, GRID_W - WIN_W)
    valid = (cp >= cs) & (cp < cs + WIN_W)
    n_dj = 2 * WIN_W - 1
    n_di = 2 * WIN_H - 1
    onehot = ((cp - c + (WIN_W - 1))[None] == np.arange(n_dj)[:, None, None]) & valid[None]
    toe = jnp.einsum("hdj,jcx->hcdx", rpb.astype(F32), jnp.asarray(onehot, F32),
                     precision=lax.Precision.HIGHEST)
    toe = toe + jnp.asarray(np.where(valid, 0.0, NEG_BIG), F32)[None, :, None, :]
    toe = toe.reshape(N_NAT_HEADS, GRID_W, n_di * GRID_W)
    tab = jnp.stack([toe[:, :, (WIN_H - 1 - dl) * GRID_W:(WIN_H - 1 - dl) * GRID_W + WIN_TOKENS]
                     for dl in range(WIN_H)], axis=1)
    return tab.astype(BF16)


def _nat_call(p_nat, bias_tab):
    b, t, _ = p_nat.shape
    n_rows = t // GRID_W
    rb = math.gcd(n_rows, NAT_ROWS)
    return pl.pallas_call(
        functools.partial(_nat_kernel, n_rows, rb),
        out_shape=jax.ShapeDtypeStruct((b, t, D_NAT), BF16),
        grid=(b, n_rows // rb),
        in_specs=[pl.BlockSpec((1, rb * GRID_W, D_NAT), lambda bi, i: (bi, i, 0)),
                  pl.BlockSpec((1, t, D_NAT), lambda bi, i: (bi, 0, 1)),
                  pl.BlockSpec((1, t, D_NAT), lambda bi, i: (bi, 0, 2)),
                  _full(bias_tab.shape)],
        out_specs=pl.BlockSpec((1, rb * GRID_W, D_NAT), lambda bi, i: (bi, i, 0)),
        compiler_params=_params(("parallel", "arbitrary")),
        name="nat",
    )(p_nat, p_nat, p_nat, bias_tab)


def _merge_xattn_kernel(batch_starts, *refs):
    n_src = len(batch_starts)
    x_refs = refs[:n_src]
    (yr_ref, yn_ref, gate_ref, kv_ref, wbr_ref, wbn_ref, wout_ref,
     gx_ref, wq_ref, wo_ref, out_ref) = refs[n_src:]
    bi = pl.program_id(0)
    x_in = x_refs[0][0]
    for s in range(1, n_src):
        x_in = jnp.where(bi >= batch_starts[s], x_refs[s][0], x_in)
    gates = gate_ref[0].astype(F32)
    mixed = (_sigmoid(gates[:, :D_MODEL]) * _dot(yr_ref[0], wbr_ref[...])
             + _sigmoid(gates[:, D_MODEL:]) * _dot(yn_ref[0], wbn_ref[...]))
    x1 = x_in + _dot(mixed.astype(BF16), wout_ref[...])
    q = _dot(_rmsnorm(x1, gx_ref[...]).astype(BF16), wq_ref[...])
    scale = XATTN_HEAD_DIM ** -0.5
    heads = []
    for h in range(N_XATTN_HEADS):
        sl = slice(h * XATTN_HEAD_DIM, (h + 1) * XATTN_HEAD_DIM)
        qh = (q[:, sl] * scale).astype(BF16)
        kh = kv_ref[0, :, sl]
        vh = kv_ref[0, :, D_MODEL + h * XATTN_HEAD_DIM:D_MODEL + (h + 1) * XATTN_HEAD_DIM]
        s = _dot_nt(qh, kh)
        m = jnp.max(s, axis=-1, keepdims=True)
        pr = jnp.exp(s - m)
        l = jnp.sum(pr, axis=-1, keepdims=True)
        heads.append(_dot(pr.astype(BF16), vh) / l)
    o = jnp.concatenate(heads, axis=1).astype(BF16)
    out_ref[0] = x1 + _dot(o, wo_ref[...])


def _merge_xattn_call(xs, y_r, y_n, gates, kv, wbr, wbn, wout, gx, wq, wo, tm):
    b, t, _ = y_r.shape
    d = xs[0].shape[2]
    n_mem = kv.shape[1]
    tile = lambda w: pl.BlockSpec((1, tm, w), lambda bi, i: (bi, i, 0))
    ws = [wbr, wbn, wout, gx.reshape(1, d), wq, wo]
    starts = tuple(int(v) for v in np.cumsum([0] + [x.shape[0] for x in xs[:-1]]))
    x_specs = [pl.BlockSpec((1, tm, d), lambda bi, i, st=st, nb=x.shape[0]:
                            (jnp.clip(bi - st, 0, nb - 1), i, 0)) for x, st in zip(xs, starts)]
    return pl.pallas_call(
        functools.partial(_merge_xattn_kernel, starts),
        out_shape=jax.ShapeDtypeStruct((b, t, d), F32),
        grid=(b, t // tm),
        in_specs=x_specs + [tile(D_RWKV), tile(D_NAT), tile(GATE_COLS),
                            pl.BlockSpec((1, n_mem, 2 * d), lambda bi, i: (bi, 0, 0))]
        + [_full(w.shape) for w in ws],
        out_specs=tile(d),
        compiler_params=_params(("parallel", "arbitrary")),
        name="merge_xattn",
    )(*xs, y_r, y_n, gates, kv, *ws)


def _ffn_kernel(final, ff_chunk, tile_starts, x_ref, g_ref, w1_ref, w2_ref, gf_ref, *out_refs):
    x = x_ref[...]
    h = _rmsnorm(x, g_ref[...]).astype(BF16)
    acc = x
    for j in range(D_FF // ff_chunk):
        sl = slice(j * ff_chunk, (j + 1) * ff_chunk)
        hf = jnp.maximum(_dot(h, w1_ref[:, sl]), 0.0)
        acc = acc + _dot((hf * hf).astype(BF16), w2_ref[sl, :])
    if final:
        acc = _rmsnorm(acc, gf_ref[...])
    if len(out_refs) == 1:
        out_refs[0][...] = acc
        return
    i = pl.program_id(0)
    bounds = list(tile_starts) + [None]
    for k, out_ref in enumerate(out_refs):
        lo, hi = bounds[k], bounds[k + 1]
        cond = (i >= lo) if hi is None else ((i >= lo) & (i < hi))

        @pl.when(cond)
        def _(out_ref=out_ref):
            out_ref[...] = acc


def _ffn_call(x2d, g, w1, w2, g_final, final, tm, out_rows):
    n, d = x2d.shape
    single = pl.Buffered(1)
    starts = _tile_starts(out_rows, tm)
    out_specs = [pl.BlockSpec((tm, d), lambda i, st=st, nt=r // tm: (jnp.clip(i - st, 0, nt - 1), 0))
                 for r, st in zip(out_rows, starts)]
    return pl.pallas_call(
        functools.partial(_ffn_kernel, final, 1024, starts),
        out_shape=[jax.ShapeDtypeStruct((r, d), F32) for r in out_rows],
        grid=(n // tm,),
        in_specs=[pl.BlockSpec((tm, d), lambda i: (i, 0)), _full((1, d)),
                  pl.BlockSpec(w1.shape, lambda i: (0, 0), pipeline_mode=single),
                  pl.BlockSpec(w2.shape, lambda i: (0, 0), pipeline_mode=single),
                  _full((1, d))],
        out_specs=out_specs,
        compiler_params=_params(("arbitrary",)),
        name="ffn",
    )(x2d, g.reshape(1, d), w1, w2, g_final.reshape(1, d))


def _pad_dir(w, d):
    z = jnp.zeros_like(w[d])
    parts = [w[0], z] if d == 0 else [z, w[1]]
    return jnp.concatenate(parts, axis=0).astype(BF16)


def _trunk(xs, mem, norm_mix, w_in, mu_prev, mu_next, w0, w_up, a0, a_up, g_up, k_k, k_a, r_k,
           gn_g, gn_b, rpb, w_br_rwkv, w_br_nat, w_out, norm_x, norm_mem, w_xq, w_xkv, w_xo,
           norm_ff, w_ff1, w_ff2, norm_final, tm=256):
    t, d = xs[0].shape[1:]
    batches = [x.shape[0] for x in xs]
    b = sum(batches)
    n_mem = mem.shape[1]
    depth = w_in.shape[0]
    head_id = np.arange(2 * LANES) // HEAD_DIM
    seg = jnp.asarray(head_id[:, None] == head_id[None, :], dtype=BF16)
    row = lambda a: a.reshape(1, -1).astype(F32)
    mem2d = mem.reshape(b * n_mem, d)
    for l in range(depth):
        w_in_l = w_in[l].astype(BF16)
        p_rw, p_nat, p_gate = _in_proj(
            [x.reshape(-1, d) for x in xs], norm_mix[l], row(mu_prev[l]), row(mu_next[l]),
            w_in_l[:, :RWKV_COLS], w_in_l[:, RWKV_COLS:RWKV_COLS + NAT_COLS],
            w_in_l[:, RWKV_COLS + NAT_COLS:], t, tm)
        p_rw = p_rw.reshape(b, t, RWKV_COLS)
        p_nat = p_nat.reshape(b, t, NAT_COLS)
        p_gate = p_gate.reshape(b, t, GATE_COLS)

        shared = [row(k_k[l]), row(k_a[l]), seg]
        dir_params = lambda dd: [row(w0[l, dd]), _pad_dir(w_up[l], dd), row(a0[l, dd]),
                                 _pad_dir(a_up[l], dd)]
        y_f = _wkv_call(False, p_rw, shared, dir_params(0), None)
        extra = [y_f, row(a0[l, 0]), _pad_dir(a_up[l], 0), g_up[l].astype(BF16), row(r_k[l]),
                 row(gn_g[l]), row(gn_b[l])]
        y_r = _wkv_call(True, p_rw, shared, dir_params(1), extra)

        y_n = _nat_call(p_nat, _nat_bias_table(rpb[l]))

        (kv,) = _norm_proj(mem2d, norm_mem[l], [w_xkv[l].astype(BF16)], [BF16], tm, "mem_kv")
        kv = kv.reshape(b, n_mem, 2 * d)
        x = _merge_xattn_call(xs, y_r, y_n, p_gate, kv, w_br_rwkv[l].astype(BF16),
                              w_br_nat[l].astype(BF16), w_out[l].astype(BF16), norm_x[l],
                              w_xq[l].astype(BF16), w_xo[l].astype(BF16), tm)
        last = l == depth - 1
        outs = _ffn_call(x.reshape(b * t, d), norm_ff[l], w_ff1[l].astype(BF16),
                         w_ff2[l].astype(BF16), norm_final, last, tm,
                         [bk * t for bk in batches] if last else [b * t])
        xs = [o.reshape(-1, t, d) for o in outs]
    return xs


def kernel(x_prompt, x_sample, mem_prompt, mem_sample, norm_mix, w_in, mu_prev, mu_next, w0, w_up, a0, a_up, g_up, k_k, k_a, r_k, gn_g, gn_b, rpb, w_br_rwkv, w_br_nat, w_out, norm_x, norm_mem, w_xq, w_xkv, w_xo, norm_ff, w_ff1, w_ff2, norm_final):
    assert x_prompt.shape[1:] == x_sample.shape[1:] and mem_prompt.shape[1:] == mem_sample.shape[1:]
    mem = jnp.concatenate([mem_prompt, mem_sample], axis=0)
    y_prompt, y_sample = _trunk(
        [x_prompt, x_sample], mem, norm_mix, w_in, mu_prev, mu_next, w0, w_up, a0, a_up, g_up, k_k,
        k_a, r_k, gn_g, gn_b, rpb, w_br_rwkv, w_br_nat, w_out, norm_x, norm_mem, w_xq, w_xkv, w_xo,
        norm_ff, w_ff1, w_ff2, norm_final)
    return (y_prompt, y_sample)
```

```python
import functools
import math

import numpy as np
import jax
import jax.numpy as jnp
from jax import lax
from jax.experimental import pallas as pl
from jax.experimental.pallas import tpu as pltpu

F32 = jnp.float32
BF16 = jnp.bfloat16

D_MODEL = 1024
GRID_W = 64
HEAD_DIM = 64
D_RWKV = 512
D_NAT = 512
N_NAT_HEADS = D_NAT // HEAD_DIM
LORA_W = 64
LORA_A = 64
LORA_G = 128
WIN_H = 8
WIN_W = 16
N_XATTN_HEADS = 4
XATTN_HEAD_DIM = D_MODEL // N_XATTN_HEADS
D_FF = 4 * D_MODEL
NORM_EPS = 1e-6
GN_EPS = 1e-5 * HEAD_DIM
RWKV_COLS = 3 * D_RWKV + 2 * LORA_W + 2 * LORA_A + LORA_G
NAT_COLS = 3 * D_NAT
GATE_COLS = 2 * D_MODEL
COL_WD = 3 * D_RWKV
COL_AD = COL_WD + 2 * LORA_W
COL_GD = COL_AD + 2 * LORA_A

LANES = 128
N_PAIRS = D_RWKV // LANES
CHUNK = 64
ROW_TILE = 16
WKV_SEQS = 4
NAT_ROWS = 4
WIN_TOKENS = WIN_H * GRID_W
NEG_BIG = -1e30
VMEM_LIMIT = 56 * 1024 * 1024


def _dot(a, b):
    return jnp.dot(a, b, preferred_element_type=F32)


def _dot_nt(a, b):
    return lax.dot_general(a, b, (((1,), (1,)), ((), ())), preferred_element_type=F32)


def _dot_tn(a, b):
    return lax.dot_general(a, b, (((0,), (0,)), ((), ())), preferred_element_type=F32)


def _dot_split_lhs(a, b):
    hi = b.astype(BF16)
    lo = (b - hi.astype(F32)).astype(BF16)
    return _dot(a, hi) + _dot(a, lo)


def _sigmoid(x):
    return 0.5 * jnp.tanh(0.5 * x) + 0.5


def _rmsnorm(x, g):
    return x * lax.rsqrt(jnp.mean(x * x, axis=-1, keepdims=True) + NORM_EPS) * g


def _params(sem):
    return pltpu.CompilerParams(dimension_semantics=sem, vmem_limit_bytes=VMEM_LIMIT)


def _full(shape):
    nd = len(shape)
    return pl.BlockSpec(shape, lambda *_: (0,) * nd, pipeline_mode=pl.Buffered(1))


def _norm_proj_kernel(n_out, x_ref, g_ref, *refs):
    w_refs, o_refs = refs[:n_out], refs[n_out:]
    h = _rmsnorm(x_ref[...], g_ref[...]).astype(BF16)
    for w_ref, o_ref in zip(w_refs, o_refs):
        o_ref[...] = _dot(h, w_ref[...]).astype(o_ref.dtype)


def _norm_proj(x2d, g, ws, out_dtypes, tm, name):
    n, d = x2d.shape
    n_out = len(ws)
    return pl.pallas_call(
        functools.partial(_norm_proj_kernel, n_out),
        out_shape=[jax.ShapeDtypeStruct((n, w.shape[1]), dt) for w, dt in zip(ws, out_dtypes)],
        grid=(n // tm,),
        in_specs=[pl.BlockSpec((tm, d), lambda i: (i, 0)), _full((1, d))]
        + [_full(w.shape) for w in ws],
        out_specs=[pl.BlockSpec((tm, w.shape[1]), lambda i: (i, 0)) for w in ws],
        compiler_params=_params(("parallel",)),
        name=name,
    )(x2d, g.reshape(1, d), *ws)


def _tile_starts(row_counts, tm):
    return tuple(int(s) // tm for s in np.cumsum([0] + list(row_counts[:-1])))


def _in_proj_kernel(tiles_per_seq, tile_starts, *refs):
    n_src = len(tile_starts)
    src_refs = refs[:3 * n_src]
    (g_ref, mup_ref, mun_ref, wrw_ref, wnat_ref, wgate_ref,
     prw_ref, pnat_ref, pgate_ref) = refs[3 * n_src:]
    i = pl.program_id(0)

    def pick(k):
        val = src_refs[k][...]
        for s in range(1, n_src):
            val = jnp.where(i >= tile_starts[s], src_refs[3 * s + k][...], val)
        return val

    x = pick(0)
    tm = x.shape[0]
    g = g_ref[...]
    h = _rmsnorm(x, g)
    hb = h.astype(BF16)
    pnat_ref[...] = _dot(hb, wnat_ref[...]).astype(pnat_ref.dtype)
    pgate_ref[...] = _dot(hb, wgate_ref[...]).astype(pgate_ref.dtype)
    pos = i % tiles_per_seq
    keep_prev = jnp.where(pos == 0, 0.0, 1.0)
    keep_next = jnp.where(pos == tiles_per_seq - 1, 0.0, 1.0)
    h_ext = jnp.concatenate([_rmsnorm(pick(1), g) * keep_prev, h,
                             _rmsnorm(pick(2), g) * keep_next], axis=0)
    p = _dot(h_ext.astype(BF16), wrw_ref[...])
    p_mid = p[8:8 + tm]
    p_prev = pltpu.roll(p, 1, 0)[8:8 + tm]
    p_next = pltpu.roll(p, tm + 15, 0)[8:8 + tm]
    prw_ref[...] = p_mid + mup_ref[...] * (p_prev - p_mid) + mun_ref[...] * (p_next - p_mid)


def _in_proj(xs, g, mup, mun, w_rw, w_nat, w_gate, seq_len, tm):
    d = xs[0].shape[1]
    n = sum(x.shape[0] for x in xs)
    per8 = tm // 8
    ws = [w_rw, w_nat, w_gate]
    starts = _tile_starts([x.shape[0] for x in xs], tm)
    src_specs, src_args = [], []
    for x, st in zip(xs, starts):
        nt = x.shape[0] // tm
        local = lambda i, st=st, nt=nt: jnp.clip(i - st, 0, nt - 1)
        last8 = x.shape[0] // 8 - 1
        src_specs += [
            pl.BlockSpec((tm, d), lambda i, local=local: (local(i), 0)),
            pl.BlockSpec((8, d), lambda i, local=local: (jnp.maximum(local(i) * per8 - 1, 0), 0)),
            pl.BlockSpec((8, d), lambda i, local=local, last8=last8:
                         (jnp.minimum((local(i) + 1) * per8, last8), 0))]
        src_args += [x, x, x]
    return pl.pallas_call(
        functools.partial(_in_proj_kernel, seq_len // tm, starts),
        out_shape=[jax.ShapeDtypeStruct((n, RWKV_COLS), F32),
                   jax.ShapeDtypeStruct((n, NAT_COLS), BF16),
                   jax.ShapeDtypeStruct((n, GATE_COLS), BF16)],
        grid=(n // tm,),
        in_specs=src_specs + [_full((1, d)), _full(mup.shape), _full(mun.shape)]
        + [_full(w.shape) for w in ws],
        out_specs=[pl.BlockSpec((tm, w.shape[1]), lambda i: (i, 0)) for w in ws],
        compiler_params=_params(("parallel",)),
        name="in_proj",
    )(*src_args, g.reshape(1, d), mup, mun, *ws)


def _seg_sum(x, seg):
    w = seg.shape[0]
    return jnp.concatenate([_dot(x[:, j:j + w].astype(BF16), seg) for j in range(0, x.shape[1], w)],
                           axis=1)


def _wkv_kernel(reverse, n_chunks, nb, *refs):
    if reverse:
        (p_ref, w0_ref, wup_ref, a0_ref, aup_ref,
         kk_ref, ka_ref, seg_ref, yf_ref, a0o_ref, aupo_ref, gup_ref, rk_ref, gng_ref, gnb_ref,
         out_ref, s_ref) = refs
    else:
        (p_ref, w0_ref, wup_ref, a0_ref, aup_ref,
         kk_ref, ka_ref, seg_ref, out_ref, s_ref) = refs

    c = pl.program_id(1)
    rows = nb * CHUNK

    @pl.when(c == 0)
    def _():
        s_ref[...] = jnp.zeros_like(s_ref)

    ps = p_ref[...].reshape(rows, RWKV_COLS)
    r = ps[:, 0:D_RWKV]
    k = ps[:, D_RWKV:2 * D_RWKV]
    v = ps[:, 2 * D_RWKV:3 * D_RWKV]
    wd = ps[:, COL_WD:COL_AD]
    ad = ps[:, COL_AD:COL_GD]
    seg = seg_ref[...]

    w_raw = w0_ref[...] + _dot(jnp.tanh(wd).astype(BF16), wup_ref[...])
    logw = (-math.exp(-0.5)) * _sigmoid(w_raw)
    ad_b = ad.astype(BF16)
    a_d = _sigmoid(a0_ref[...] + _dot(ad_b, aup_ref[...]))
    kkv = k * kk_ref[...]
    kkn = kkv * lax.rsqrt(jnp.maximum(_seg_sum(kkv * kkv, seg), 1e-24))
    k_d = k * (1.0 + (a_d - 1.0) * ka_ref[...])
    b_vec = kkn * a_d

    ti = lax.broadcasted_iota(jnp.int32, (CHUNK, CHUNK), 0)
    si = lax.broadcasted_iota(jnp.int32, (CHUNK, CHUNK), 1)
    tri = ((si >= ti) if reverse else (si <= ti)).astype(BF16)
    last = 0 if reverse else CHUNK - 1
    cums, totals = [], []
    for i in range(nb):
        cum_i = _dot_split_lhs(tri, logw[i * CHUNK:(i + 1) * CHUNK])
        cums.append(cum_i)
        totals.append(cum_i[last:last + 1])
    cum = jnp.concatenate(cums, axis=0) if nb > 1 else cums[0]
    p_tots = [jnp.exp(tt) for tt in totals]
    a_t = (-kkn) * jnp.exp(cum - logw)
    r_t = r * jnp.exp(cum)
    e_inv = jnp.exp(-cum)
    b_t = b_vec * e_inv
    k_t = k_d * e_inv

    ri = lax.broadcasted_iota(jnp.int32, (2 * CHUNK, LANES), 0)
    li = lax.broadcasted_iota(jnp.int32, (2 * CHUNK, LANES), 1)
    ri2 = lax.broadcasted_iota(jnp.int32, (4 * CHUNK, LANES), 0)
    li2 = lax.broadcasted_iota(jnp.int32, (4 * CHUNK, LANES), 1)
    t_idx = ri2 % CHUNK
    s_idx = li2 % CHUNK
    incl = (ri2 // CHUNK) % 2
    sc_mask2 = (s_idx > t_idx - incl) if reverse else (s_idx < t_idx + incl)
    lane = lax.broadcasted_iota(jnp.int32, (1, LANES), 1)
    head_lanes = (lane < HEAD_DIM, lane >= HEAD_DIM)
    bd_mask = (ri // HEAD_DIM) == (li // HEAD_DIM)
    zeros_cv = jnp.zeros((CHUNK, LANES), BF16)

    tasks = [(i, g) for i in range(nb) for g in range(N_PAIRS)]
    n_tasks = len(tasks)
    n_steps = int(math.log2(CHUNK))

    def tile(a, i, g):
        return a[i * CHUNK:(i + 1) * CHUNK, g * LANES:(g + 1) * LANES]

    lhs_f = [jnp.concatenate([tile(a_t, i, g), tile(r_t, i, g)], axis=0) for i, g in tasks]
    rhs_f = [jnp.concatenate([tile(b_t, i, g), tile(k_t, i, g)], axis=0) for i, g in tasks]
    rhs = [rf.astype(BF16) for rf in rhs_f]
    p_tot = [p_tots[i][:, g * LANES:(g + 1) * LANES] for i, g in tasks]
    bk_o = [(rf * pt).astype(BF16) for rf, pt in zip(rhs_f, p_tot)]
    v_g = [tile(v, i, g).astype(BF16) for i, g in tasks]
    state = [s_ref[i, g] for i, g in tasks]
    from_state = [_dot_nt(lf.astype(BF16), st.astype(BF16)) for lf, st in zip(lhs_f, state)]
    sc2 = [jnp.where(sc_mask2, _dot_nt(jnp.concatenate(
        [jnp.where(head_lanes[e], lf, 0.0) for e in range(2)], axis=0).astype(BF16), rh), 0.0)
           for lf, rh in zip(lhs_f, rhs)]
    top = [[s2[2 * CHUNK * e:2 * CHUNK * e + CHUNK] for e in range(2)] for s2 in sc2]
    bot = [[s2[2 * CHUNK * e + CHUNK:2 * CHUNK * (e + 1)] for e in range(2)] for s2 in sc2]
    zv = [jnp.concatenate([zeros_cv, vg], axis=0) for vg in v_g]
    akv = [_dot(jnp.concatenate(top[n], axis=0).astype(BF16), zv[n]) for n in range(n_tasks)]
    x0 = [from_state[n][:CHUNK] + jnp.where(head_lanes[0], akv[n][:CHUNK], akv[n][CHUNK:])
          for n in range(n_tasks)]

    a_pad = [[jnp.where(head_lanes[0], top[n][e], 0.0) for e in range(2)]
             for n in range(n_tasks)]
    xs = [[x0[n], x0[n]] for n in range(n_tasks)]
    for it in range(n_steps):
        skip = ((1 << it) // ROW_TILE) * ROW_TILE
        lo, hi = (0, CHUNK - skip) if reverse else (skip, CHUNK)
        pad_rows = lambda m: jnp.concatenate(
            ([jnp.zeros((lo, LANES), F32)] if lo else []) + [m]
            + ([jnp.zeros((CHUNK - hi, LANES), F32)] if hi < CHUNK else []), axis=0)
        add_rows = lambda x, dx: jnp.concatenate(
            ([x[:lo]] if lo else []) + [x[lo:hi] + dx] + ([x[hi:]] if hi < CHUNK else []), axis=0)
        for n in range(n_tasks):
            for e in range(2):
                a_sq = a_pad[n][e][lo:hi, :CHUNK].astype(BF16)
                if it < n_steps - 1:
                    z = jnp.concatenate([a_pad[n][e], xs[n][e]], axis=1).astype(BF16)
                    rz = _dot(a_sq, z)
                    a_pad[n][e] = pad_rows(rz[:, :LANES])
                    xs[n][e] = add_rows(xs[n][e], rz[:, LANES:])
                else:
                    xs[n][e] = add_rows(xs[n][e], _dot(a_sq, xs[n][e].astype(BF16)))

    ys = [[None] * N_PAIRS for _ in range(nb)]
    for n, (i, g) in enumerate(tasks):
        u_g = jnp.where(head_lanes[0], xs[n][0], xs[n][1])
        uv = jnp.concatenate([u_g.astype(BF16), v_g[n]], axis=0)
        rbk = _dot(jnp.concatenate(bot[n], axis=0).astype(BF16), uv)
        ys[i][g] = from_state[n][CHUNK:] + jnp.where(head_lanes[0], rbk[:CHUNK], rbk[CHUNK:])
        upd = _dot_tn(uv, bk_o[n])
        s_ref[i, g] = state[n] * p_tot[n] + jnp.where(bd_mask, upd, 0.0)

    y = jnp.concatenate([jnp.concatenate(ys[i], axis=1) for i in range(nb)], axis=0)
    if not reverse:
        out_ref[...] = y.reshape(nb, CHUNK, D_RWKV)
        return

    out = yf_ref[...].reshape(rows, D_RWKV) + y
    inv_n = 1.0 / HEAD_DIM
    mean = _seg_sum(out, seg) * inv_n
    cen = out - mean
    var = _seg_sum(cen * cen, seg) * inv_n
    out = cen * lax.rsqrt(var + GN_EPS) * gng_ref[...] + gnb_ref[...]
    a_o = _sigmoid(a0o_ref[...] + _dot(ad_b, aupo_ref[...]))
    k_sum = k_d + k * (1.0 + (a_o - 1.0) * ka_ref[...])
    bonus = _seg_sum(r * k_sum * rk_ref[...], seg)
    out = out + bonus * v
    gd = ps[:, COL_GD:RWKV_COLS]
    gate = _dot(_sigmoid(gd).astype(BF16), gup_ref[...])
    out_ref[...] = (out * gate).reshape(nb, CHUNK, D_RWKV).astype(out_ref.dtype)


def _wkv_call(reverse, p_rw, shared, dir_params, extra):
    b, t, _ = p_rw.shape
    n_chunks = t // CHUNK
    nb = math.gcd(b, WKV_SEQS)
    cur = lambda bi, c: (bi, (n_chunks - 1 - c) if reverse else c, 0)
    args = [p_rw] + list(dir_params) + list(shared)
    in_specs = [pl.BlockSpec((nb, CHUNK, RWKV_COLS), cur)] + [_full(a.shape) for a in args[1:]]
    if reverse:
        y_f = extra[0]
        args += [y_f] + list(extra[1:])
        in_specs += [pl.BlockSpec((nb, CHUNK, D_RWKV), cur)] + [_full(a.shape) for a in extra[1:]]
        out_dtype = BF16
    else:
        out_dtype = F32
    return pl.pallas_call(
        functools.partial(_wkv_kernel, reverse, n_chunks, nb),
        out_shape=jax.ShapeDtypeStruct((b, t, D_RWKV), out_dtype),
        grid=(b // nb, n_chunks),
        in_specs=in_specs,
        out_specs=pl.BlockSpec((nb, CHUNK, D_RWKV), cur),
        scratch_shapes=[pltpu.VMEM((nb, N_PAIRS, LANES, LANES), F32)],
        compiler_params=_params(("parallel", "arbitrary")),
        name="wkv_bwd" if reverse else "wkv_fwd",
    )(*args)


def _nat_kernel(n_rows, rb, q_ref, k_ref, v_ref, bias_ref, out_ref):
    i0 = pl.program_id(1) * rb
    lane = lax.broadcasted_iota(jnp.int32, (1, LANES), 1)
    head_lanes = (lane < HEAD_DIM, lane >= HEAD_DIM)
    scale = HEAD_DIM ** -0.5
    starts, deltas = [], []
    for j in range(rb):
        rs = jnp.clip(i0 + j - WIN_H // 2, 0, n_rows - WIN_H)
        starts.append(pl.multiple_of(rs * GRID_W, GRID_W))
        deltas.append(i0 + j - rs)
    tasks = [(j, g) for j in range(rb) for g in range(N_PAIRS)]
    lanes = lambda g: slice(g * LANES, (g + 1) * LANES)
    kw = [k_ref[0, pl.ds(starts[j], WIN_TOKENS), lanes(g)] for j, g in tasks]
    vw = [v_ref[0, pl.ds(starts[j], WIN_TOKENS), lanes(g)] for j, g in tasks]
    qg = [q_ref[0, j * GRID_W:(j + 1) * GRID_W, lanes(g)].astype(F32) * scale for j, g in tasks]
    s = [[_dot_nt(jnp.where(head_lanes[e], qg[n], 0.0).astype(BF16), kw[n])
          + bias_ref[2 * g + e, deltas[j]].astype(F32) for e in range(2)]
         for n, (j, g) in enumerate(tasks)]
    pr = [[jnp.exp(s[n][e] - jnp.max(s[n][e], axis=-1, keepdims=True)) for e in range(2)]
          for n in range(len(tasks))]
    o = [[_dot(pr[n][e].astype(BF16), vw[n]) / jnp.sum(pr[n][e], axis=-1, keepdims=True)
          for e in range(2)] for n in range(len(tasks))]
    for n, (j, g) in enumerate(tasks):
        out_ref[0, j * GRID_W:(j + 1) * GRID_W, lanes(g)] = jnp.where(
            head_lanes[0], o[n][0], o[n][1]).astype(out_ref.dtype)


def _nat_bias_table(rpb):
    c = np.arange(GRID_W)[:, None]
    cp = np.arange(GRID_W)[None, :]
    cs = np.clip(c - WIN_W // 2, 0, GRID_W - WIN_W)
    valid = (cp >= cs) & (cp < cs + WIN_W)
    n_dj = 2 * WIN_W - 1
    n_di = 2 * WIN_H - 1
    onehot = ((cp - c + (WIN_W - 1))[None] == np.arange(n_dj)[:, None, None]) & valid[None]
    toe = jnp.einsum("hdj,jcx->hcdx", rpb.astype(F32), jnp.asarray(onehot, F32),
                     precision=lax.Precision.HIGHEST)
    toe = toe + jnp.asarray(np.where(valid, 0.0, NEG_BIG), F32)[None, :, None, :]
    toe = toe.reshape(N_NAT_HEADS, GRID_W, n_di * GRID_W)
    tab = jnp.stack([toe[:, :, (WIN_H - 1 - dl) * GRID_W:(WIN_H - 1 - dl) * GRID_W + WIN_TOKENS]
                     for dl in range(WIN_H)], axis=1)
    return tab.astype(BF16)


def _nat_call(p_nat, bias_tab):
    b, t, _ = p_nat.shape
    n_rows = t // GRID_W
    rb = math.gcd(n_rows, NAT_ROWS)
    return pl.pallas_call(
        functools.partial(_nat_kernel, n_rows, rb),
        out_shape=jax.ShapeDtypeStruct((b, t, D_NAT), BF16),
        grid=(b, n_rows // rb),
        in_specs=[pl.BlockSpec((1, rb * GRID_W, D_NAT), lambda bi, i: (bi, i, 0)),
                  pl.BlockSpec((1, t, D_NAT), lambda bi, i: (bi, 0, 1)),
                  pl.BlockSpec((1, t, D_NAT), lambda bi, i: (bi, 0, 2)),
                  _full(bias_tab.shape)],
        out_specs=pl.BlockSpec((1, rb * GRID_W, D_NAT), lambda bi, i: (bi, i, 0)),
        compiler_params=_params(("parallel", "arbitrary")),
        name="nat",
    )(p_nat, p_nat, p_nat, bias_tab)


def _merge_xattn_kernel(batch_starts, *refs):
    n_src = len(batch_starts)
    x_refs = refs[:n_src]
    (yr_ref, yn_ref, gate_ref, kv_ref, wbr_ref, wbn_ref, wout_ref,
     gx_ref, wq_ref, wo_ref, out_ref) = refs[n_src:]
    bi = pl.program_id(0)
    x_in = x_refs[0][0]
    for s in range(1, n_src):
        x_in = jnp.where(bi >= batch_starts[s], x_refs[s][0], x_in)
    gates = gate_ref[0].astype(F32)
    mixed = (_sigmoid(gates[:, :D_MODEL]) * _dot(yr_ref[0], wbr_ref[...])
             + _sigmoid(gates[:, D_MODEL:]) * _dot(yn_ref[0], wbn_ref[...]))
    x1 = x_in + _dot(mixed.astype(BF16), wout_ref[...])
    q = _dot(_rmsnorm(x1, gx_ref[...]).astype(BF16), wq_ref[...])
    scale = XATTN_HEAD_DIM ** -0.5
    heads = []
    for h in range(N_XATTN_HEADS):
        sl = slice(h * XATTN_HEAD_DIM, (h + 1) * XATTN_HEAD_DIM)
        qh = (q[:, sl] * scale).astype(BF16)
        kh = kv_ref[0, :, sl]
        vh = kv_ref[0, :, D_MODEL + h * XATTN_HEAD_DIM:D_MODEL + (h + 1) * XATTN_HEAD_DIM]
        s = _dot_nt(qh, kh)
        m = jnp.max(s, axis=-1, keepdims=True)
        pr = jnp.exp(s - m)
        l = jnp.sum(pr, axis=-1, keepdims=True)
        heads.append(_dot(pr.astype(BF16), vh) / l)
    o = jnp.concatenate(heads, axis=1).astype(BF16)
    out_ref[0] = x1 + _dot(o, wo_ref[...])


def _merge_xattn_call(xs, y_r, y_n, gates, kv, wbr, wbn, wout, gx, wq, wo, tm):
    b, t, _ = y_r.shape
    d = xs[0].shape[2]
    n_mem = kv.shape[1]
    tile = lambda w: pl.BlockSpec((1, tm, w), lambda bi, i: (bi, i, 0))
    ws = [wbr, wbn, wout, gx.reshape(1, d), wq, wo]
    starts = tuple(int(v) for v in np.cumsum([0] + [x.shape[0] for x in xs[:-1]]))
    x_specs = [pl.BlockSpec((1, tm, d), lambda bi, i, st=st, nb=x.shape[0]:
                            (jnp.clip(bi - st, 0, nb - 1), i, 0)) for x, st in zip(xs, starts)]
    return pl.pallas_call(
        functools.partial(_merge_xattn_kernel, starts),
        out_shape=jax.ShapeDtypeStruct((b, t, d), F32),
        grid=(b, t // tm),
        in_specs=x_specs + [tile(D_RWKV), tile(D_NAT), tile(GATE_COLS),
                            pl.BlockSpec((1, n_mem, 2 * d), lambda bi, i: (bi, 0, 0))]
        + [_full(w.shape) for w in ws],
        out_specs=tile(d),
        compiler_params=_params(("parallel", "arbitrary")),
        name="merge_xattn",
    )(*xs, y_r, y_n, gates, kv, *ws)


def _ffn_kernel(final, ff_chunk, tile_starts, x_ref, g_ref, w1_ref, w2_ref, gf_ref, *out_refs):
    x = x_ref[...]
    h = _rmsnorm(x, g_ref[...]).astype(BF16)
    acc = x
    for j in range(D_FF // ff_chunk):
        sl = slice(j * ff_chunk, (j + 1) * ff_chunk)
        hf = jnp.maximum(_dot(h, w1_ref[:, sl]), 0.0)
        acc = acc + _dot((hf * hf).astype(BF16), w2_ref[sl, :])
    if final:
        acc = _rmsnorm(acc, gf_ref[...])
    if len(out_refs) == 1:
        out_refs[0][...] = acc
        return
    i = pl.program_id(0)
    bounds = list(tile_starts) + [None]
    for k, out_ref in enumerate(out_refs):
        lo, hi = bounds[k], bounds[k + 1]
        cond = (i >= lo) if hi is None else ((i >= lo) & (i < hi))

        @pl.when(cond)
        def _(out_ref=out_ref):
            out_ref[...] = acc


def _ffn_call(x2d, g, w1, w2, g_final, final, tm, out_rows):
    n, d = x2d.shape
    starts = _tile_starts(out_rows, tm)
    out_specs = [pl.BlockSpec((tm, d), lambda i, st=st, nt=r // tm: (jnp.clip(i - st, 0, nt - 1), 0))
                 for r, st in zip(out_rows, starts)]
    return pl.pallas_call(
        functools.partial(_ffn_kernel, final, 1024, starts),
        out_shape=[jax.ShapeDtypeStruct((r, d), F32) for r in out_rows],
        grid=(n // tm,),
        in_specs=[pl.BlockSpec((tm, d), lambda i: (i, 0)), _full((1, d)),
                  _full(w1.shape), _full(w2.shape), _full((1, d))],
        out_specs=out_specs,
        compiler_params=_params(("arbitrary",)),
        name="ffn",
    )(x2d, g.reshape(1, d), w1, w2, g_final.reshape(1, d))


def _pad_dir(w, d):
    z = jnp.zeros_like(w[d])
    parts = [w[0], z] if d == 0 else [z, w[1]]
    return jnp.concatenate(parts, axis=0).astype(BF16)


def _trunk(xs, mem, norm_mix, w_in, mu_prev, mu_next, w0, w_up, a0, a_up, g_up, k_k, k_a, r_k,
           gn_g, gn_b, rpb, w_br_rwkv, w_br_nat, w_out, norm_x, norm_mem, w_xq, w_xkv, w_xo,
           norm_ff, w_ff1, w_ff2, norm_final, tm=512):
    t, d = xs[0].shape[1:]
    batches = [x.shape[0] for x in xs]
    b = sum(batches)
    n_mem = mem.shape[1]
    depth = w_in.shape[0]
    head_id = np.arange(2 * LANES) // HEAD_DIM
    seg = jnp.asarray(head_id[:, None] == head_id[None, :], dtype=BF16)
    row = lambda a: a.reshape(1, -1).astype(F32)
    mem2d = mem.reshape(b * n_mem, d)
    for l in range(depth):
        w_in_l = w_in[l].astype(BF16)
        p_rw, p_nat, p_gate = _in_proj(
            [x.reshape(-1, d) for x in xs], norm_mix[l], row(mu_prev[l]), row(mu_next[l]),
            w_in_l[:, :RWKV_COLS], w_in_l[:, RWKV_COLS:RWKV_COLS + NAT_COLS],
            w_in_l[:, RWKV_COLS + NAT_COLS:], t, tm)
        p_rw = p_rw.reshape(b, t, RWKV_COLS)
        p_nat = p_nat.reshape(b, t, NAT_COLS)
        p_gate = p_gate.reshape(b, t, GATE_COLS)

        shared = [row(k_k[l]), row(k_a[l]), seg]
        dir_params = lambda dd: [row(w0[l, dd]), _pad_dir(w_up[l], dd), row(a0[l, dd]),
                                 _pad_dir(a_up[l], dd)]
        y_f = _wkv_call(False, p_rw, shared, dir_params(0), None)
        extra = [y_f, row(a0[l, 0]), _pad_dir(a_up[l], 0), g_up[l].astype(BF16), row(r_k[l]),
                 row(gn_g[l]), row(gn_b[l])]
        y_r = _wkv_call(True, p_rw, shared, dir_params(1), extra)

        y_n = _nat_call(p_nat, _nat_bias_table(rpb[l]))

        (kv,) = _norm_proj(mem2d, norm_mem[l], [w_xkv[l].astype(BF16)], [BF16], tm, "mem_kv")
        kv = kv.reshape(b, n_mem, 2 * d)
        x = _merge_xattn_call(xs, y_r, y_n, p_gate, kv, w_br_rwkv[l].astype(BF16),
                              w_br_nat[l].astype(BF16), w_out[l].astype(BF16), norm_x[l],
                              w_xq[l].astype(BF16), w_xo[l].astype(BF16), tm)
        last = l == depth - 1
        outs = _ffn_call(x.reshape(b * t, d), norm_ff[l], w_ff1[l].astype(BF16),
                         w_ff2[l].astype(BF16), norm_final, last, tm,
                         [bk * t for bk in batches] if last else [b * t])
        xs = [o.reshape(-1, t, d) for o in outs]
    return xs


def kernel(x_prompt, x_sample, mem_prompt, mem_sample, norm_mix, w_in, mu_prev, mu_next, w0, w_up, a0, a_up, g_up, k_k, k_a, r_k, gn_g, gn_b, rpb, w_br_rwkv, w_br_nat, w_out, norm_x, norm_mem, w_xq, w_xkv, w_xo, norm_ff, w_ff1, w_ff2, norm_final):
    assert x_prompt.shape[1:] == x_sample.shape[1:] and mem_prompt.shape[1:] == mem_sample.shape[1:]
    mem = jnp.concatenate([mem_prompt, mem_sample], axis=0)
    y_prompt, y_sample = _trunk(
        [x_prompt, x_sample], mem, norm_mix, w_in, mu_prev, mu_next, w0, w_up, a0, a_up, g_up, k_k,
        k_a, r_k, gn_g, gn_b, rpb, w_br_rwkv, w_br_nat, w_out, norm_x, norm_mem, w_xq, w_xkv, w_xo,
        norm_ff, w_ff1, w_ff2, norm_final)
    return (y_prompt, y_sample)
```

```python
import functools
import math

import numpy as np
import jax
import jax.numpy as jnp
from jax import lax
from jax.experimental import pallas as pl
from jax.experimental.pallas import tpu as pltpu

F32 = jnp.float32
BF16 = jnp.bfloat16

D_MODEL = 1024
GRID_W = 64
HEAD_DIM = 64
D_RWKV = 512
D_NAT = 512
N_NAT_HEADS = D_NAT // HEAD_DIM
LORA_W = 64
LORA_A = 64
LORA_G = 128
WIN_H = 8
WIN_W = 16
N_XATTN_HEADS = 4
XATTN_HEAD_DIM = D_MODEL // N_XATTN_HEADS
D_FF = 4 * D_MODEL
NORM_EPS = 1e-6
GN_EPS = 1e-5 * HEAD_DIM
RWKV_COLS = 3 * D_RWKV + 2 * LORA_W + 2 * LORA_A + LORA_G
NAT_COLS = 3 * D_NAT
GATE_COLS = 2 * D_MODEL
COL_WD = 3 * D_RWKV
COL_AD = COL_WD + 2 * LORA_W
COL_GD = COL_AD + 2 * LORA_A

LANES = 128
N_PAIRS = D_RWKV // LANES
CHUNK = 64
ROW_TILE = 16
WKV_SEQS = 4
NAT_ROWS = 4
WIN_TOKENS = WIN_H * GRID_W
NEG_BIG = -1e30
LOG2_E = math.log2(math.e)
VMEM_LIMIT = 56 * 1024 * 1024


def _dot(a, b):
    return jnp.dot(a, b, preferred_element_type=F32)


def _dot_nt(a, b):
    return lax.dot_general(a, b, (((1,), (1,)), ((), ())), preferred_element_type=F32)


def _dot_tn(a, b):
    return lax.dot_general(a, b, (((0,), (0,)), ((), ())), preferred_element_type=F32)


def _dot_split_lhs(a, b):
    hi = b.astype(BF16)
    lo = (b - hi.astype(F32)).astype(BF16)
    return _dot(a, hi) + _dot(a, lo)


def _sigmoid_of_half(xh):
    return 0.5 * jnp.tanh(xh) + 0.5


def _sigmoid(x):
    return _sigmoid_of_half(0.5 * x)


def _rmsnorm(x, g):
    return x * lax.rsqrt(jnp.mean(x * x, axis=-1, keepdims=True) + NORM_EPS) * g


def _params(sem):
    return pltpu.CompilerParams(dimension_semantics=sem, vmem_limit_bytes=VMEM_LIMIT)


def _full(shape):
    nd = len(shape)
    return pl.BlockSpec(shape, lambda *_: (0,) * nd, pipeline_mode=pl.Buffered(1))


def _norm_proj_kernel(n_out, x_ref, g_ref, *refs):
    w_refs, o_refs = refs[:n_out], refs[n_out:]
    h = _rmsnorm(x_ref[...], g_ref[...]).astype(BF16)
    for w_ref, o_ref in zip(w_refs, o_refs):
        o_ref[...] = _dot(h, w_ref[...]).astype(o_ref.dtype)


def _norm_proj(x2d, g, ws, out_dtypes, tm, name):
    n, d = x2d.shape
    tm = math.gcd(n, tm)
    n_out = len(ws)
    return pl.pallas_call(
        functools.partial(_norm_proj_kernel, n_out),
        out_shape=[jax.ShapeDtypeStruct((n, w.shape[1]), dt) for w, dt in zip(ws, out_dtypes)],
        grid=(n // tm,),
        in_specs=[pl.BlockSpec((tm, d), lambda i: (i, 0)), _full((1, d))]
        + [_full(w.shape) for w in ws],
        out_specs=[pl.BlockSpec((tm, w.shape[1]), lambda i: (i, 0)) for w in ws],
        compiler_params=_params(("parallel",)),
        name=name,
    )(x2d, g.reshape(1, d), *ws)


def _tile_starts(row_counts, tm):
    return tuple(int(s) // tm for s in np.cumsum([0] + list(row_counts[:-1])))


def _in_proj_kernel(tiles_per_seq, tile_starts, *refs):
    n_src = len(tile_starts)
    src_refs = refs[:3 * n_src]
    (g_ref, mup_ref, mun_ref, wrw_ref, wnat_ref, wgate_ref,
     prw_ref, pnat_ref, pgate_ref) = refs[3 * n_src:]
    i = pl.program_id(0)

    def pick(k):
        val = src_refs[k][...]
        for s in range(1, n_src):
            val = jnp.where(i >= tile_starts[s], src_refs[3 * s + k][...], val)
        return val

    x = pick(0)
    tm = x.shape[0]
    g = g_ref[...]
    h = _rmsnorm(x, g)
    pos = i % tiles_per_seq
    keep_prev = jnp.where(pos == 0, 0.0, 1.0)
    keep_next = jnp.where(pos == tiles_per_seq - 1, 0.0, 1.0)
    h_ext = jnp.concatenate([_rmsnorm(pick(1), g) * keep_prev, h,
                             _rmsnorm(pick(2), g) * keep_next], axis=0)
    p = _dot(h_ext.astype(BF16), wrw_ref[...])
    p_mid = p[8:8 + tm]
    p_prev = pltpu.roll(p, 1, 0)[8:8 + tm]
    p_next = pltpu.roll(p, tm + 15, 0)[8:8 + tm]
    prw_ref[...] = p_mid + mup_ref[...] * (p_prev - p_mid) + mun_ref[...] * (p_next - p_mid)
    hb = h.astype(BF16)
    pnat_ref[...] = _dot(hb, wnat_ref[...]).astype(pnat_ref.dtype)
    pgate_ref[...] = _dot(hb, wgate_ref[...]).astype(pgate_ref.dtype)


def _in_proj(xs, g, mup, mun, w_rw, w_nat, w_gate, seq_len, tm):
    d = xs[0].shape[1]
    n = sum(x.shape[0] for x in xs)
    per8 = tm // 8
    ws = [w_rw, w_nat, w_gate]
    starts = _tile_starts([x.shape[0] for x in xs], tm)
    src_specs, src_args = [], []
    for x, st in zip(xs, starts):
        nt = x.shape[0] // tm
        local = lambda i, st=st, nt=nt: jnp.clip(i - st, 0, nt - 1)
        last8 = x.shape[0] // 8 - 1
        src_specs += [
            pl.BlockSpec((tm, d), lambda i, local=local: (local(i), 0)),
            pl.BlockSpec((8, d), lambda i, local=local: (jnp.maximum(local(i) * per8 - 1, 0), 0)),
            pl.BlockSpec((8, d), lambda i, local=local, last8=last8:
                         (jnp.minimum((local(i) + 1) * per8, last8), 0))]
        src_args += [x, x, x]
    return pl.pallas_call(
        functools.partial(_in_proj_kernel, seq_len // tm, starts),
        out_shape=[jax.ShapeDtypeStruct((n, RWKV_COLS), F32),
                   jax.ShapeDtypeStruct((n, NAT_COLS), BF16),
                   jax.ShapeDtypeStruct((n, GATE_COLS), BF16)],
        grid=(n // tm,),
        in_specs=src_specs + [_full((1, d)), _full(mup.shape), _full(mun.shape)]
        + [_full(w.shape) for w in ws],
        out_specs=[pl.BlockSpec((tm, w.shape[1]), lambda i: (i, 0)) for w in ws],
        compiler_params=_params(("parallel",)),
        name="in_proj",
    )(*src_args, g.reshape(1, d), mup, mun, *ws)


def _seg_sum(x, seg):
    w = seg.shape[0]
    return jnp.concatenate([_dot(x[:, j:j + w].astype(BF16), seg) for j in range(0, x.shape[1], w)],
                           axis=1)


def _wkv_kernel(reverse, n_chunks, nb, *refs):
    if reverse:
        (p_ref, w0_ref, wup_ref, a0_ref, aup_ref,
         kk_ref, ka_ref, seg_ref, yf_ref, a0o_ref, aupo_ref, gup_ref, rk_ref, gng_ref, gnb_ref,
         out_ref, s_ref) = refs
    else:
        (p_ref, w0_ref, wup_ref, a0_ref, aup_ref,
         kk_ref, ka_ref, seg_ref, out_ref, s_ref) = refs

    c = pl.program_id(1)
    rows = nb * CHUNK

    @pl.when(c == 0)
    def _():
        s_ref[...] = jnp.zeros_like(s_ref)

    ps = p_ref[...].reshape(rows, RWKV_COLS)
    r = ps[:, 0:D_RWKV]
    k = ps[:, D_RWKV:2 * D_RWKV]
    v = ps[:, 2 * D_RWKV:3 * D_RWKV]
    wd = ps[:, COL_WD:COL_AD]
    ad = ps[:, COL_AD:COL_GD]
    seg = seg_ref[...]

    half_w = w0_ref[...] + _dot(jnp.tanh(wd).astype(BF16), wup_ref[...])
    logw = (-math.exp(-0.5) * LOG2_E) * _sigmoid_of_half(half_w)
    ad_b = ad.astype(BF16)
    a_d = _sigmoid_of_half(a0_ref[...] + _dot(ad_b, aup_ref[...]))
    kkv = k * kk_ref[...]
    kkn = kkv * lax.rsqrt(jnp.maximum(_seg_sum(kkv * kkv, seg), 1e-24))
    k_d = k * (1.0 + (a_d - 1.0) * ka_ref[...])
    b_vec = kkn * a_d

    ti = lax.broadcasted_iota(jnp.int32, (CHUNK, CHUNK), 0)
    si = lax.broadcasted_iota(jnp.int32, (CHUNK, CHUNK), 1)
    tri = ((si >= ti) if reverse else (si <= ti)).astype(BF16)
    last = 0 if reverse else CHUNK - 1
    cums, totals = [], []
    for i in range(nb):
        cum_i = _dot_split_lhs(tri, logw[i * CHUNK:(i + 1) * CHUNK])
        cums.append(cum_i)
        totals.append(cum_i[last:last + 1])
    cum = jnp.concatenate(cums, axis=0) if nb > 1 else cums[0]
    p_tots = [jnp.exp2(tt) for tt in totals]
    a_t = (-kkn) * jnp.exp2(cum - logw)
    r_t = r * jnp.exp2(cum)
    e_inv = jnp.exp2(-cum)
    b_t = b_vec * e_inv
    k_t = k_d * e_inv
    if reverse:
        a_o = _sigmoid_of_half(a0o_ref[...] + _dot(ad_b, aupo_ref[...]))
        k_sum = k_d + k * (1.0 + (a_o - 1.0) * ka_ref[...])
        bonus_v = _seg_sum(r * k_sum * rk_ref[...], seg) * v
        gate = _dot(_sigmoid(ps[:, COL_GD:RWKV_COLS]).astype(BF16), gup_ref[...])

    ri = lax.broadcasted_iota(jnp.int32, (2 * CHUNK, LANES), 0)
    li = lax.broadcasted_iota(jnp.int32, (2 * CHUNK, LANES), 1)
    ri2 = lax.broadcasted_iota(jnp.int32, (4 * CHUNK, LANES), 0)
    li2 = lax.broadcasted_iota(jnp.int32, (4 * CHUNK, LANES), 1)
    t_idx = ri2 % CHUNK
    s_idx = li2 % CHUNK
    incl = (ri2 // CHUNK) % 2
    sc_mask2 = (s_idx > t_idx - incl) if reverse else (s_idx < t_idx + incl)
    lane = lax.broadcasted_iota(jnp.int32, (1, LANES), 1)
    head_lanes = (lane < HEAD_DIM, lane >= HEAD_DIM)
    bd_mask = (ri // HEAD_DIM) == (li // HEAD_DIM)
    zeros_cv = jnp.zeros((CHUNK, LANES), BF16)

    tasks = [(i, g) for i in range(nb) for g in range(N_PAIRS)]
    n_tasks = len(tasks)
    n_steps = int(math.log2(CHUNK))

    def tile(a, i, g):
        return a[i * CHUNK:(i + 1) * CHUNK, g * LANES:(g + 1) * LANES]

    lhs_f = [jnp.concatenate([tile(a_t, i, g), tile(r_t, i, g)], axis=0) for i, g in tasks]
    rhs_f = [jnp.concatenate([tile(b_t, i, g), tile(k_t, i, g)], axis=0) for i, g in tasks]
    rhs = [rf.astype(BF16) for rf in rhs_f]
    p_tot = [p_tots[i][:, g * LANES:(g + 1) * LANES] for i, g in tasks]
    bk_o = [(rf * pt).astype(BF16) for rf, pt in zip(rhs_f, p_tot)]
    v_g = [tile(v, i, g).astype(BF16) for i, g in tasks]
    state = [s_ref[i, g] for i, g in tasks]
    from_state = [_dot_nt(lf.astype(BF16), st.astype(BF16)) for lf, st in zip(lhs_f, state)]
    sc2 = [jnp.where(sc_mask2, _dot_nt(jnp.concatenate(
        [jnp.where(head_lanes[e], lf, 0.0) for e in range(2)], axis=0).astype(BF16), rh), 0.0)
           for lf, rh in zip(lhs_f, rhs)]
    top = [[s2[2 * CHUNK * e:2 * CHUNK * e + CHUNK] for e in range(2)] for s2 in sc2]
    bot = [[s2[2 * CHUNK * e + CHUNK:2 * CHUNK * (e + 1)] for e in range(2)] for s2 in sc2]
    zv = [jnp.concatenate([zeros_cv, vg], axis=0) for vg in v_g]
    akv = [_dot(jnp.concatenate(top[n], axis=0).astype(BF16), zv[n]) for n in range(n_tasks)]
    x0 = [from_state[n][:CHUNK] + jnp.where(head_lanes[0], akv[n][:CHUNK], akv[n][CHUNK:])
          for n in range(n_tasks)]

    a_pad = [[jnp.where(head_lanes[0], top[n][e], 0.0) for e in range(2)]
             for n in range(n_tasks)]
    xs = [[x0[n], x0[n]] for n in range(n_tasks)]
    for it in range(n_steps):
        skip = ((1 << it) // ROW_TILE) * ROW_TILE
        lo, hi = (0, CHUNK - skip) if reverse else (skip, CHUNK)
        pad_rows = lambda m: jnp.concatenate(
            ([jnp.zeros((lo, LANES), F32)] if lo else []) + [m]
            + ([jnp.zeros((CHUNK - hi, LANES), F32)] if hi < CHUNK else []), axis=0)
        add_rows = lambda x, dx: jnp.concatenate(
            ([x[:lo]] if lo else []) + [x[lo:hi] + dx] + ([x[hi:]] if hi < CHUNK else []), axis=0)
        for n in range(n_tasks):
            for e in range(2):
                a_sq = a_pad[n][e][lo:hi, :CHUNK].astype(BF16)
                if it < n_steps - 1:
                    z = jnp.concatenate([a_pad[n][e], xs[n][e]], axis=1).astype(BF16)
                    rz = _dot(a_sq, z)
                    a_pad[n][e] = pad_rows(rz[:, :LANES])
                    xs[n][e] = add_rows(xs[n][e], rz[:, LANES:])
                else:
                    xs[n][e] = add_rows(xs[n][e], _dot(a_sq, xs[n][e].astype(BF16)))

    ys = [[None] * N_PAIRS for _ in range(nb)]
    for n, (i, g) in enumerate(tasks):
        u_g = jnp.where(head_lanes[0], xs[n][0], xs[n][1])
        uv = jnp.concatenate([u_g.astype(BF16), v_g[n]], axis=0)
        rbk = _dot(jnp.concatenate(bot[n], axis=0).astype(BF16), uv)
        ys[i][g] = from_state[n][CHUNK:] + jnp.where(head_lanes[0], rbk[:CHUNK], rbk[CHUNK:])
        upd = _dot_tn(uv, bk_o[n])
        s_ref[i, g] = state[n] * p_tot[n] + jnp.where(bd_mask, upd, 0.0)

    y = jnp.concatenate([jnp.concatenate(ys[i], axis=1) for i in range(nb)], axis=0)
    if not reverse:
        out_ref[...] = y.reshape(nb, CHUNK, D_RWKV)
        return

    out = yf_ref[...].reshape(rows, D_RWKV) + y
    inv_n = 1.0 / HEAD_DIM
    mean = _seg_sum(out, seg) * inv_n
    cen = out - mean
    var = _seg_sum(cen * cen, seg) * inv_n
    out = cen * lax.rsqrt(var + GN_EPS) * gng_ref[...] + gnb_ref[...]
    out_ref[...] = ((out + bonus_v) * gate).reshape(nb, CHUNK, D_RWKV).astype(out_ref.dtype)


def _wkv_call(reverse, p_rw, shared, dir_params, extra):
    b, t, _ = p_rw.shape
    n_chunks = t // CHUNK
    nb = math.gcd(b, WKV_SEQS)
    cur = lambda bi, c: (bi, (n_chunks - 1 - c) if reverse else c, 0)
    args = [p_rw] + list(dir_params) + list(shared)
    in_specs = [pl.BlockSpec((nb, CHUNK, RWKV_COLS), cur)] + [_full(a.shape) for a in args[1:]]
    if reverse:
        y_f = extra[0]
        args += [y_f] + list(extra[1:])
        in_specs += [pl.BlockSpec((nb, CHUNK, D_RWKV), cur)] + [_full(a.shape) for a in extra[1:]]
        out_dtype = BF16
    else:
        out_dtype = F32
    return pl.pallas_call(
        functools.partial(_wkv_kernel, reverse, n_chunks, nb),
        out_shape=jax.ShapeDtypeStruct((b, t, D_RWKV), out_dtype),
        grid=(b // nb, n_chunks),
        in_specs=in_specs,
        out_specs=pl.BlockSpec((nb, CHUNK, D_RWKV), cur),
        scratch_shapes=[pltpu.VMEM((nb, N_PAIRS, LANES, LANES), F32)],
        compiler_params=_params(("parallel", "arbitrary")),
        name="wkv_bwd" if reverse else "wkv_fwd",
    )(*args)


def _nat_kernel(n_rows, rb, q_ref, k_ref, v_ref, bias_ref, out_ref):
    i0 = pl.program_id(1) * rb
    lane = lax.broadcasted_iota(jnp.int32, (1, LANES), 1)
    head_lanes = (lane < HEAD_DIM, lane >= HEAD_DIM)
    starts, deltas = [], []
    for j in range(rb):
        rs = jnp.clip(i0 + j - WIN_H // 2, 0, n_rows - WIN_H)
        starts.append(pl.multiple_of(rs * GRID_W, GRID_W))
        deltas.append(i0 + j - rs)
    tasks = [(j, g) for j in range(rb) for g in range(N_PAIRS)]
    lanes = lambda g: slice(g * LANES, (g + 1) * LANES)
    kw = [k_ref[0, pl.ds(starts[j], WIN_TOKENS), lanes(g)] for j, g in tasks]
    vw = [v_ref[0, pl.ds(starts[j], WIN_TOKENS), lanes(g)] for j, g in tasks]
    qg = [q_ref[0, j * GRID_W:(j + 1) * GRID_W, lanes(g)].astype(F32) for j, g in tasks]
    s = [_dot_nt(jnp.concatenate([jnp.where(head_lanes[e], qg[n], 0.0) for e in range(2)],
                                 axis=0).astype(BF16), kw[n])
         + jnp.concatenate([bias_ref[2 * g + e, deltas[j]] for e in range(2)], axis=0)
         for n, (j, g) in enumerate(tasks)]
    pr = [jnp.exp2(sn - jnp.max(sn, axis=-1, keepdims=True)) for sn in s]
    o = [_dot(pr[n].astype(BF16), vw[n]) * (1.0 / jnp.sum(pr[n], axis=-1, keepdims=True))
         for n in range(len(tasks))]
    for n, (j, g) in enumerate(tasks):
        out_ref[0, j * GRID_W:(j + 1) * GRID_W, lanes(g)] = jnp.where(
            head_lanes[0], o[n][:GRID_W], o[n][GRID_W:]).astype(out_ref.dtype)


def _nat_bias_table(rpb):
    c = np.arange(GRID_W)[:, None]
    cp = np.arange(GRID_W)[None, :]
    cs = np.clip(c - WIN_W // 2, 0, GRID_W - WIN_W)
    valid = (cp >= cs) & (cp < cs + WIN_W)
    n_dj = 2 * WIN_W - 1
    n_di = 2 * WIN_H - 1
    onehot = ((cp - c + (WIN_W - 1))[None] == np.arange(n_dj)[:, None, None]) & valid[None]
    toe = jnp.einsum("hdj,jcx->hcdx", rpb.astype(F32), jnp.asarray(onehot, F32),
                     precision=lax.Precision.HIGHEST)
    toe = toe * LOG2_E + jnp.asarray(np.where(valid, 0.0, NEG_BIG), F32)[None, :, None, :]
    toe = toe.reshape(N_NAT_HEADS, GRID_W, n_di * GRID_W)
    tab = jnp.stack([toe[:, :, (WIN_H - 1 - dl) * GRID_W:(WIN_H - 1 - dl) * GRID_W + WIN_TOKENS]
                     for dl in range(WIN_H)], axis=1)
    return tab


def _nat_call(p_nat, bias_tab):
    b, t, _ = p_nat.shape
    n_rows = t // GRID_W
    rb = math.gcd(n_rows, NAT_ROWS)
    return pl.pallas_call(
        functools.partial(_nat_kernel, n_rows, rb),
        out_shape=jax.ShapeDtypeStruct((b, t, D_NAT), BF16),
        grid=(b, n_rows // rb),
        in_specs=[pl.BlockSpec((1, rb * GRID_W, D_NAT), lambda bi, i: (bi, i, 0)),
                  pl.BlockSpec((1, t, D_NAT), lambda bi, i: (bi, 0, 1)),
                  pl.BlockSpec((1, t, D_NAT), lambda bi, i: (bi, 0, 2)),
                  _full(bias_tab.shape)],
        out_specs=pl.BlockSpec((1, rb * GRID_W, D_NAT), lambda bi, i: (bi, i, 0)),
        compiler_params=_params(("parallel", "arbitrary")),
        name="nat",
    )(p_nat, p_nat, p_nat, bias_tab)


def _merge_xattn_kernel(batch_starts, *refs):
    n_src = len(batch_starts)
    x_refs = refs[:n_src]
    (yr_ref, yn_ref, gate_ref, kv_ref, wbr_ref, wbn_ref, wout_ref,
     gx_ref, wq_ref, wo_ref, out_ref) = refs[n_src:]
    bi = pl.program_id(0)
    x_in = x_refs[0][0]
    for s in range(1, n_src):
        x_in = jnp.where(bi >= batch_starts[s], x_refs[s][0], x_in)
    gates = gate_ref[0].astype(F32)
    mixed = (_sigmoid(gates[:, :D_MODEL]) * _dot(yr_ref[0], wbr_ref[...])
             + _sigmoid(gates[:, D_MODEL:]) * _dot(yn_ref[0], wbn_ref[...]))
    x1 = x_in + _dot(mixed.astype(BF16), wout_ref[...])
    q = _dot(_rmsnorm(x1, gx_ref[...]).astype(BF16), wq_ref[...])
    scale = XATTN_HEAD_DIM ** -0.5
    heads = []
    for h in range(N_XATTN_HEADS):
        sl = slice(h * XATTN_HEAD_DIM, (h + 1) * XATTN_HEAD_DIM)
        qh = (q[:, sl] * scale).astype(BF16)
        kh = kv_ref[0, :, sl]
        vh = kv_ref[0, :, D_MODEL + h * XATTN_HEAD_DIM:D_MODEL + (h + 1) * XATTN_HEAD_DIM]
        s = _dot_nt(qh, kh)
        m = jnp.max(s, axis=-1, keepdims=True)
        pr = jnp.exp(s - m)
        inv_l = 1.0 / jnp.sum(pr, axis=-1, keepdims=True)
        heads.append(_dot(pr.astype(BF16), vh) * inv_l)
    o = jnp.concatenate(heads, axis=1).astype(BF16)
    out_ref[0] = x1 + _dot(o, wo_ref[...])


def _merge_xattn_call(xs, y_r, y_n, gates, kv, wbr, wbn, wout, gx, wq, wo, tm):
    b, t, _ = y_r.shape
    d = xs[0].shape[2]
    n_mem = kv.shape[1]
    tile = lambda w: pl.BlockSpec((1, tm, w), lambda bi, i: (bi, i, 0))
    ws = [wbr, wbn, wout, gx.reshape(1, d), wq, wo]
    starts = tuple(int(v) for v in np.cumsum([0] + [x.shape[0] for x in xs[:-1]]))
    x_specs = [pl.BlockSpec((1, tm, d), lambda bi, i, st=st, nb=x.shape[0]:
                            (jnp.clip(bi - st, 0, nb - 1), i, 0)) for x, st in zip(xs, starts)]
    return pl.pallas_call(
        functools.partial(_merge_xattn_kernel, starts),
        out_shape=jax.ShapeDtypeStruct((b, t, d), F32),
        grid=(b, t // tm),
        in_specs=x_specs + [tile(D_RWKV), tile(D_NAT), tile(GATE_COLS),
                            pl.BlockSpec((1, n_mem, 2 * d), lambda bi, i: (bi, 0, 0))]
        + [_full(w.shape) for w in ws],
        out_specs=tile(d),
        compiler_params=_params(("parallel", "arbitrary")),
        name="merge_xattn",
    )(*xs, y_r, y_n, gates, kv, *ws)


def _ffn_kernel(final, ff_chunk, tile_starts, x_ref, g_ref, w1_ref, w2_ref, gf_ref, *out_refs):
    x = x_ref[...]
    h = _rmsnorm(x, g_ref[...]).astype(BF16)
    acc = x
    for j in range(D_FF // ff_chunk):
        sl = slice(j * ff_chunk, (j + 1) * ff_chunk)
        hf = jnp.maximum(_dot(h, w1_ref[:, sl]), 0.0)
        acc = acc + _dot((hf * hf).astype(BF16), w2_ref[sl, :])
    if final:
        acc = _rmsnorm(acc, gf_ref[...])
    if len(out_refs) == 1:
        out_refs[0][...] = acc
        return
    i = pl.program_id(0)
    bounds = list(tile_starts) + [None]
    for k, out_ref in enumerate(out_refs):
        lo, hi = bounds[k], bounds[k + 1]
        cond = (i >= lo) if hi is None else ((i >= lo) & (i < hi))

        @pl.when(cond)
        def _(out_ref=out_ref):
            out_ref[...] = acc


def _ffn_call(x2d, g, w1, w2, g_final, final, tm, out_rows):
    n, d = x2d.shape
    starts = _tile_starts(out_rows, tm)
    out_specs = [pl.BlockSpec((tm, d), lambda i, st=st, nt=r // tm: (jnp.clip(i - st, 0, nt - 1), 0))
                 for r, st in zip(out_rows, starts)]
    return pl.pallas_call(
        functools.partial(_ffn_kernel, final, 1024, starts),
        out_shape=[jax.ShapeDtypeStruct((r, d), F32) for r in out_rows],
        grid=(n // tm,),
        in_specs=[pl.BlockSpec((tm, d), lambda i: (i, 0)), _full((1, d)),
                  _full(w1.shape), _full(w2.shape), _full((1, d))],
        out_specs=out_specs,
        compiler_params=_params(("arbitrary",)),
        name="ffn",
    )(x2d, g.reshape(1, d), w1, w2, g_final.reshape(1, d))


def _pad_dir(w, d):
    z = jnp.zeros_like(w[d])
    parts = [w[0], z] if d == 0 else [z, w[1]]
    return jnp.concatenate(parts, axis=0).astype(BF16)


def _trunk(xs, mem, norm_mix, w_in, mu_prev, mu_next, w0, w_up, a0, a_up, g_up, k_k, k_a, r_k,
           gn_g, gn_b, rpb, w_br_rwkv, w_br_nat, w_out, norm_x, norm_mem, w_xq, w_xkv, w_xo,
           norm_ff, w_ff1, w_ff2, norm_final, tm=512):
    t, d = xs[0].shape[1:]
    assert d == D_MODEL and t % tm == 0 and t % (NAT_ROWS * GRID_W) == 0 and t >= WIN_TOKENS
    batches = [x.shape[0] for x in xs]
    b = sum(batches)
    n_mem = mem.shape[1]
    depth = w_in.shape[0]
    head_id = np.arange(2 * LANES) // HEAD_DIM
    seg = jnp.asarray(head_id[:, None] == head_id[None, :], dtype=BF16)
    row = lambda a: a.reshape(1, -1).astype(F32)
    mem2d = mem.reshape(b * n_mem, d)
    for l in range(depth):
        col_scale = np.ones((1, w_in.shape[2]), np.float32)
        col_scale[:, RWKV_COLS:RWKV_COLS + D_NAT] = HEAD_DIM ** -0.5 * LOG2_E
        w_in_l = (w_in[l] * col_scale).astype(BF16)
        p_rw, p_nat, p_gate = _in_proj(
            [x.reshape(-1, d) for x in xs], norm_mix[l], row(mu_prev[l]), row(mu_next[l]),
            w_in_l[:, :RWKV_COLS], w_in_l[:, RWKV_COLS:RWKV_COLS + NAT_COLS],
            w_in_l[:, RWKV_COLS + NAT_COLS:], t, tm)
        p_rw = p_rw.reshape(b, t, RWKV_COLS)
        p_nat = p_nat.reshape(b, t, NAT_COLS)
        p_gate = p_gate.reshape(b, t, GATE_COLS)

        shared = [row(k_k[l]), row(k_a[l]), seg]
        dir_params = lambda dd: [row(0.5 * w0[l, dd]), _pad_dir(0.5 * w_up[l], dd),
                                 row(0.5 * a0[l, dd]), _pad_dir(0.5 * a_up[l], dd)]
        y_f = _wkv_call(False, p_rw, shared, dir_params(0), None)
        extra = [y_f, row(0.5 * a0[l, 0]), _pad_dir(0.5 * a_up[l], 0), g_up[l].astype(BF16), row(r_k[l]),
                 row(gn_g[l]), row(gn_b[l])]
        y_r = _wkv_call(True, p_rw, shared, dir_params(1), extra)

        y_n = _nat_call(p_nat, _nat_bias_table(rpb[l]))

        (kv,) = _norm_proj(mem2d, norm_mem[l], [w_xkv[l].astype(BF16)], [BF16], tm, "mem_kv")
        kv = kv.reshape(b, n_mem, 2 * d)
        x = _merge_xattn_call(xs, y_r, y_n, p_gate, kv, w_br_rwkv[l].astype(BF16),
                              w_br_nat[l].astype(BF16), w_out[l].astype(BF16), norm_x[l],
                              w_xq[l].astype(BF16), w_xo[l].astype(BF16), tm)
        last = l == depth - 1
        outs = _ffn_call(x.reshape(b * t, d), norm_ff[l], w_ff1[l].astype(BF16),
                         w_ff2[l].astype(BF16), norm_final, last, tm,
                         [bk * t for bk in batches] if last else [b * t])
        xs = [o.reshape(-1, t, d) for o in outs]
    return xs


def kernel(x_prompt, x_sample, mem_prompt, mem_sample, norm_mix, w_in, mu_prev, mu_next, w0, w_up, a0, a_up, g_up, k_k, k_a, r_k, gn_g, gn_b, rpb, w_br_rwkv, w_br_nat, w_out, norm_x, norm_mem, w_xq, w_xkv, w_xo, norm_ff, w_ff1, w_ff2, norm_final):
    assert x_prompt.shape[1:] == x_sample.shape[1:] and mem_prompt.shape[1:] == mem_sample.shape[1:]
    mem = jnp.concatenate([mem_prompt, mem_sample], axis=0)
    y_prompt, y_sample = _trunk(
        [x_prompt, x_sample], mem, norm_mix, w_in, mu_prev, mu_next, w0, w_up, a0, a_up, g_up, k_k,
        k_a, r_k, gn_g, gn_b, rpb, w_br_rwkv, w_br_nat, w_out, norm_x, norm_mem, w_xq, w_xkv, w_xo,
        norm_ff, w_ff1, w_ff2, norm_final)
    return (y_prompt, y_sample)
```

```python
import functools
import math

import numpy as np
import jax
import jax.numpy as jnp
from jax import lax
from jax.experimental import pallas as pl
from jax.experimental.pallas import tpu as pltpu

F32 = jnp.float32
BF16 = jnp.bfloat16

D_MODEL = 1024
GRID_W = 64
HEAD_DIM = 64
D_RWKV = 512
D_NAT = 512
N_NAT_HEADS = D_NAT // HEAD_DIM
LORA_W = 64
LORA_A = 64
LORA_G = 128
WIN_H = 8
WIN_W = 16
N_XATTN_HEADS = 4
XATTN_HEAD_DIM = D_MODEL // N_XATTN_HEADS
D_FF = 4 * D_MODEL
NORM_EPS = 1e-6
GN_EPS = 1e-5 * HEAD_DIM
RWKV_COLS = 3 * D_RWKV + 2 * LORA_W + 2 * LORA_A + LORA_G
NAT_COLS = 3 * D_NAT
GATE_COLS = 2 * D_MODEL
COL_WD = 3 * D_RWKV
COL_AD = COL_WD + 2 * LORA_W
COL_GD = COL_AD + 2 * LORA_A

LANES = 128
N_PAIRS = D_RWKV // LANES
CHUNK = 64
ROW_TILE = 16
WKV_SEQS = 6
NAT_ROWS = 4
WIN_TOKENS = WIN_H * GRID_W
NEG_BIG = -1e30
LOG2_E = math.log2(math.e)
VMEM_LIMIT = 56 * 1024 * 1024


def _dot(a, b):
    return jnp.dot(a, b, preferred_element_type=F32)


def _dot_nt(a, b):
    return lax.dot_general(a, b, (((1,), (1,)), ((), ())), preferred_element_type=F32)


def _dot_tn(a, b):
    return lax.dot_general(a, b, (((0,), (0,)), ((), ())), preferred_element_type=F32)


def _dot_split_lhs(a, b):
    hi = b.astype(BF16)
    lo = (b - hi.astype(F32)).astype(BF16)
    return _dot(a, hi) + _dot(a, lo)


def _sigmoid_of_half(xh):
    return 0.5 * jnp.tanh(xh) + 0.5


def _sigmoid(x):
    return _sigmoid_of_half(0.5 * x)


def _rmsnorm(x, g):
    return x * lax.rsqrt(jnp.mean(x * x, axis=-1, keepdims=True) + NORM_EPS) * g


def _params(sem):
    return pltpu.CompilerParams(dimension_semantics=sem, vmem_limit_bytes=VMEM_LIMIT)


def _full(shape):
    nd = len(shape)
    return pl.BlockSpec(shape, lambda *_: (0,) * nd, pipeline_mode=pl.Buffered(1))


def _norm_proj_kernel(n_out, x_ref, g_ref, *refs):
    w_refs, o_refs = refs[:n_out], refs[n_out:]
    h = _rmsnorm(x_ref[...], g_ref[...]).astype(BF16)
    for w_ref, o_ref in zip(w_refs, o_refs):
        o_ref[...] = _dot(h, w_ref[...]).astype(o_ref.dtype)


def _norm_proj(x2d, g, ws, out_dtypes, tm, name):
    n, d = x2d.shape
    tm = math.gcd(n, tm)
    n_out = len(ws)
    return pl.pallas_call(
        functools.partial(_norm_proj_kernel, n_out),
        out_shape=[jax.ShapeDtypeStruct((n, w.shape[1]), dt) for w, dt in zip(ws, out_dtypes)],
        grid=(n // tm,),
        in_specs=[pl.BlockSpec((tm, d), lambda i: (i, 0)), _full((1, d))]
        + [_full(w.shape) for w in ws],
        out_specs=[pl.BlockSpec((tm, w.shape[1]), lambda i: (i, 0)) for w in ws],
        compiler_params=_params(("parallel",)),
        name=name,
    )(x2d, g.reshape(1, d), *ws)


def _tile_starts(row_counts, tm):
    return tuple(int(s) // tm for s in np.cumsum([0] + list(row_counts[:-1])))


def _in_proj_kernel(tiles_per_seq, tile_starts, *refs):
    n_src = len(tile_starts)
    src_refs = refs[:3 * n_src]
    (g_ref, mup_ref, mun_ref, wrw_ref, wnat_ref, wgate_ref,
     prw_ref, pnat_ref, pgate_ref) = refs[3 * n_src:]
    i = pl.program_id(0)

    def pick(k):
        val = src_refs[k][...]
        for s in range(1, n_src):
            val = jnp.where(i >= tile_starts[s], src_refs[3 * s + k][...], val)
        return val

    x = pick(0)
    tm = x.shape[0]
    g = g_ref[...]
    h = _rmsnorm(x, g)
    pos = i % tiles_per_seq
    keep_prev = jnp.where(pos == 0, 0.0, 1.0)
    keep_next = jnp.where(pos == tiles_per_seq - 1, 0.0, 1.0)
    h_ext = jnp.concatenate([_rmsnorm(pick(1), g) * keep_prev, h,
                             _rmsnorm(pick(2), g) * keep_next], axis=0)
    p = _dot(h_ext.astype(BF16), wrw_ref[...])
    p_mid = p[8:8 + tm]
    p_prev = pltpu.roll(p, 1, 0)[8:8 + tm]
    p_next = pltpu.roll(p, tm + 15, 0)[8:8 + tm]
    prw_ref[...] = p_mid + mup_ref[...] * (p_prev - p_mid) + mun_ref[...] * (p_next - p_mid)
    hb = h.astype(BF16)
    pnat_ref[...] = _dot(hb, wnat_ref[...]).astype(pnat_ref.dtype)
    pgate_ref[...] = _dot(hb, wgate_ref[...]).astype(pgate_ref.dtype)


def _in_proj(xs, g, mup, mun, w_rw, w_nat, w_gate, seq_len, tm):
    d = xs[0].shape[1]
    n = sum(x.shape[0] for x in xs)
    per8 = tm // 8
    ws = [w_rw, w_nat, w_gate]
    starts = _tile_starts([x.shape[0] for x in xs], tm)
    src_specs, src_args = [], []
    for x, st in zip(xs, starts):
        nt = x.shape[0] // tm
        local = lambda i, st=st, nt=nt: jnp.clip(i - st, 0, nt - 1)
        last8 = x.shape[0] // 8 - 1
        src_specs += [
            pl.BlockSpec((tm, d), lambda i, local=local: (local(i), 0)),
            pl.BlockSpec((8, d), lambda i, local=local: (jnp.maximum(local(i) * per8 - 1, 0), 0)),
            pl.BlockSpec((8, d), lambda i, local=local, last8=last8:
                         (jnp.minimum((local(i) + 1) * per8, last8), 0))]
        src_args += [x, x, x]
    return pl.pallas_call(
        functools.partial(_in_proj_kernel, seq_len // tm, starts),
        out_shape=[jax.ShapeDtypeStruct((n, RWKV_COLS), F32),
                   jax.ShapeDtypeStruct((n, NAT_COLS), BF16),
                   jax.ShapeDtypeStruct((n, GATE_COLS), BF16)],
        grid=(n // tm,),
        in_specs=src_specs + [_full((1, d)), _full(mup.shape), _full(mun.shape)]
        + [_full(w.shape) for w in ws],
        out_specs=[pl.BlockSpec((tm, w.shape[1]), lambda i: (i, 0)) for w in ws],
        compiler_params=_params(("parallel",)),
        name="in_proj",
    )(*src_args, g.reshape(1, d), mup, mun, *ws)


def _seg_sum(x, seg):
    w = seg.shape[0]
    return jnp.concatenate([_dot(x[:, j:j + w].astype(BF16), seg) for j in range(0, x.shape[1], w)],
                           axis=1)


def _wkv_kernel(reverse, n_chunks, nb, *refs):
    if reverse:
        (p_ref, w0_ref, wup_ref, a0_ref, aup_ref,
         kk_ref, ka_ref, seg_ref, yf_ref, a0o_ref, aupo_ref, gup_ref, rk_ref, gng_ref, gnb_ref,
         out_ref, s_ref) = refs
    else:
        (p_ref, w0_ref, wup_ref, a0_ref, aup_ref,
         kk_ref, ka_ref, seg_ref, out_ref, s_ref) = refs

    c = pl.program_id(1)

    @pl.when(c == 0)
    def _():
        s_ref[...] = jnp.zeros_like(s_ref)

    seg = seg_ref[...]
    ti = lax.broadcasted_iota(jnp.int32, (CHUNK, CHUNK), 0)
    si = lax.broadcasted_iota(jnp.int32, (CHUNK, CHUNK), 1)
    tri = ((si >= ti) if reverse else (si <= ti)).astype(BF16)
    last = 0 if reverse else CHUNK - 1
    ri = lax.broadcasted_iota(jnp.int32, (2 * CHUNK, LANES), 0)
    li = lax.broadcasted_iota(jnp.int32, (2 * CHUNK, LANES), 1)
    ri2 = lax.broadcasted_iota(jnp.int32, (4 * CHUNK, LANES), 0)
    li2 = lax.broadcasted_iota(jnp.int32, (4 * CHUNK, LANES), 1)
    t_idx = ri2 % CHUNK
    s_idx = li2 % CHUNK
    incl = (ri2 // CHUNK) % 2
    sc_mask2 = (s_idx > t_idx - incl) if reverse else (s_idx < t_idx + incl)
    lane = lax.broadcasted_iota(jnp.int32, (1, LANES), 1)
    head_lanes = (lane < HEAD_DIM, lane >= HEAD_DIM)
    bd_mask = (ri // HEAD_DIM) == (li // HEAD_DIM)
    zeros_cv = jnp.zeros((CHUNK, LANES), BF16)
    n_steps = int(math.log2(CHUNK))

    def prepare(seqs):
        ps = jnp.concatenate([p_ref[i] for i in seqs], axis=0)
        r = ps[:, 0:D_RWKV]
        k = ps[:, D_RWKV:2 * D_RWKV]
        v = ps[:, 2 * D_RWKV:3 * D_RWKV]
        wd = ps[:, COL_WD:COL_AD]
        ad = ps[:, COL_AD:COL_GD]
        half_w = w0_ref[...] + _dot(jnp.tanh(wd).astype(BF16), wup_ref[...])
        logw = (-math.exp(-0.5) * LOG2_E) * _sigmoid_of_half(half_w)
        ad_b = ad.astype(BF16)
        a_d = _sigmoid_of_half(a0_ref[...] + _dot(ad_b, aup_ref[...]))
        kkv = k * kk_ref[...]
        kkn = kkv * lax.rsqrt(jnp.maximum(_seg_sum(kkv * kkv, seg), 1e-24))
        k_d = k * (1.0 + (a_d - 1.0) * ka_ref[...])
        b_vec = kkn * a_d
        cums = [_dot_split_lhs(tri, logw[n * CHUNK:(n + 1) * CHUNK]) for n in range(len(seqs))]
        p_tots = [jnp.exp2(cm[last:last + 1]) for cm in cums]
        cum = jnp.concatenate(cums, axis=0) if len(seqs) > 1 else cums[0]
        a_t = (-kkn) * jnp.exp2(cum - logw)
        r_t = r * jnp.exp2(cum)
        e_inv = jnp.exp2(-cum)
        prep = dict(a_t=a_t, r_t=r_t, b_t=b_vec * e_inv, k_t=k_d * e_inv, v=v, p_tots=p_tots)
        if reverse:
            a_o = _sigmoid_of_half(a0o_ref[...] + _dot(ad_b, aupo_ref[...]))
            k_sum = k_d + k * (1.0 + (a_o - 1.0) * ka_ref[...])
            prep["bonus_v"] = _seg_sum(r * k_sum * rk_ref[...], seg) * v
            prep["gate"] = _dot(_sigmoid(ps[:, COL_GD:RWKV_COLS]).astype(BF16), gup_ref[...])
        return prep

    def scan(seqs, prep):
        tasks = [(n, i, g) for n, i in enumerate(seqs) for g in range(N_PAIRS)]

        def tile(a, n, g):
            return a[n * CHUNK:(n + 1) * CHUNK, g * LANES:(g + 1) * LANES]

        lhs_f = [jnp.concatenate([tile(prep["a_t"], n, g), tile(prep["r_t"], n, g)], axis=0)
                 for n, _, g in tasks]
        rhs_f = [jnp.concatenate([tile(prep["b_t"], n, g), tile(prep["k_t"], n, g)], axis=0)
                 for n, _, g in tasks]
        rhs = [rf.astype(BF16) for rf in rhs_f]
        p_tot = [prep["p_tots"][n][:, g * LANES:(g + 1) * LANES] for n, _, g in tasks]
        bk_o = [(rf * pt).astype(BF16) for rf, pt in zip(rhs_f, p_tot)]
        v_g = [tile(prep["v"], n, g).astype(BF16) for n, _, g in tasks]
        state = [s_ref[i, g] for _, i, g in tasks]
        from_state = [_dot_nt(lf.astype(BF16), st.astype(BF16)) for lf, st in zip(lhs_f, state)]
        sc2 = [jnp.where(sc_mask2, _dot_nt(jnp.concatenate(
            [jnp.where(head_lanes[e], lf, 0.0) for e in range(2)], axis=0).astype(BF16), rh), 0.0)
               for lf, rh in zip(lhs_f, rhs)]
        top = [[s2[2 * CHUNK * e:2 * CHUNK * e + CHUNK] for e in range(2)] for s2 in sc2]
        bot = [[s2[2 * CHUNK * e + CHUNK:2 * CHUNK * (e + 1)] for e in range(2)] for s2 in sc2]
        zv = [jnp.concatenate([zeros_cv, vg], axis=0) for vg in v_g]
        akv = [_dot(jnp.concatenate(tp, axis=0).astype(BF16), z) for tp, z in zip(top, zv)]
        x0 = [fs[:CHUNK] + jnp.where(head_lanes[0], ak[:CHUNK], ak[CHUNK:])
              for fs, ak in zip(from_state, akv)]

        a_both = [jnp.where(head_lanes[0], tp[0], pltpu.roll(tp[1], HEAD_DIM, 1)) for tp in top]
        xs = list(x0)
        lane2 = lax.broadcasted_iota(jnp.int32, (1, 2 * LANES), 1) % LANES
        own2 = (lane2 < HEAD_DIM, lane2 >= HEAD_DIM)
        for it in range(n_steps):
            skip = ((1 << it) // ROW_TILE) * ROW_TILE
            lo, hi = (0, CHUNK - skip) if reverse else (skip, CHUNK)
            pad_rows = lambda m: jnp.concatenate(
                ([jnp.zeros((lo, LANES), F32)] if lo else []) + [m]
                + ([jnp.zeros((CHUNK - hi, LANES), F32)] if hi < CHUNK else []), axis=0)
            add_rows = lambda x, dx: jnp.concatenate(
                ([x[:lo]] if lo else []) + [x[lo:hi] + dx] + ([x[hi:]] if hi < CHUNK else []),
                axis=0)
            squaring = it < n_steps - 1
            own = own2 if squaring else head_lanes
            for n in range(len(tasks)):
                w = jnp.concatenate([a_both[n], xs[n]], axis=1) if squaring else xs[n]
                z = jnp.concatenate([jnp.where(own[e], w, 0.0) for e in range(2)],
                                    axis=0).astype(BF16)
                rz = _dot(a_both[n][lo:hi].astype(BF16), z)
                if squaring:
                    a_both[n] = pad_rows(rz[:, :LANES])
                    xs[n] = add_rows(xs[n], rz[:, LANES:])
                else:
                    xs[n] = add_rows(xs[n], rz)

        ys = [[None] * N_PAIRS for _ in seqs]
        for t, (n, i, g) in enumerate(tasks):
            uv = jnp.concatenate([xs[t].astype(BF16), v_g[t]], axis=0)
            rbk = _dot(jnp.concatenate(bot[t], axis=0).astype(BF16), uv)
            ys[n][g] = from_state[t][CHUNK:] + jnp.where(head_lanes[0], rbk[:CHUNK], rbk[CHUNK:])
            upd = _dot_tn(uv, bk_o[t])
            s_ref[i, g] = state[t] * p_tot[t] + jnp.where(bd_mask, upd, 0.0)
        y = jnp.concatenate([jnp.concatenate(yn, axis=1) for yn in ys], axis=0)
        if reverse:
            out = jnp.concatenate([yf_ref[i] for i in seqs], axis=0) + y
            inv_n = 1.0 / HEAD_DIM
            mean = _seg_sum(out, seg) * inv_n
            cen = out - mean
            var = _seg_sum(cen * cen, seg) * inv_n
            out = cen * lax.rsqrt(var + GN_EPS) * gng_ref[...] + gnb_ref[...]
            y = (out + prep["bonus_v"]) * prep["gate"]
        for n, i in enumerate(seqs):
            out_ref[i] = y[n * CHUNK:(n + 1) * CHUNK].astype(out_ref.dtype)

    seqs = list(range(nb))
    scan(seqs, prepare(seqs))


def _wkv_call(reverse, p_rw, shared, dir_params, extra):
    b, t, _ = p_rw.shape
    n_chunks = t // CHUNK
    nb = math.gcd(b, WKV_SEQS)
    cur = lambda bi, c: (bi, (n_chunks - 1 - c) if reverse else c, 0)
    args = [p_rw] + list(dir_params) + list(shared)
    in_specs = [pl.BlockSpec((nb, CHUNK, RWKV_COLS), cur)] + [_full(a.shape) for a in args[1:]]
    if reverse:
        y_f = extra[0]
        args += [y_f] + list(extra[1:])
        in_specs += [pl.BlockSpec((nb, CHUNK, D_RWKV), cur)] + [_full(a.shape) for a in extra[1:]]
        out_dtype = BF16
    else:
        out_dtype = F32
    return pl.pallas_call(
        functools.partial(_wkv_kernel, reverse, n_chunks, nb),
        out_shape=jax.ShapeDtypeStruct((b, t, D_RWKV), out_dtype),
        grid=(b // nb, n_chunks),
        in_specs=in_specs,
        out_specs=pl.BlockSpec((nb, CHUNK, D_RWKV), cur),
        scratch_shapes=[pltpu.VMEM((nb, N_PAIRS, LANES, LANES), F32)],
        compiler_params=_params(("parallel", "arbitrary")),
        name="wkv_bwd" if reverse else "wkv_fwd",
    )(*args)


def _nat_kernel(n_rows, rb, q_ref, k_ref, v_ref, bias_ref, out_ref):
    i0 = pl.program_id(1) * rb
    lane = lax.broadcasted_iota(jnp.int32, (1, LANES), 1)
    head_lanes = (lane < HEAD_DIM, lane >= HEAD_DIM)
    starts, deltas = [], []
    for j in range(rb):
        rs = jnp.clip(i0 + j - WIN_H // 2, 0, n_rows - WIN_H)
        starts.append(pl.multiple_of(rs * GRID_W, GRID_W))
        deltas.append(i0 + j - rs)
    tasks = [(j, g) for j in range(rb) for g in range(N_PAIRS)]
    lanes = lambda g: slice(g * LANES, (g + 1) * LANES)
    kw = [k_ref[0, pl.ds(starts[j], WIN_TOKENS), lanes(g)] for j, g in tasks]
    vw = [v_ref[0, pl.ds(starts[j], WIN_TOKENS), lanes(g)] for j, g in tasks]
    qg = [q_ref[0, j * GRID_W:(j + 1) * GRID_W, lanes(g)].astype(F32) for j, g in tasks]
    s = [_dot_nt(jnp.concatenate([jnp.where(head_lanes[e], qg[n], 0.0) for e in range(2)],
                                 axis=0).astype(BF16), kw[n])
         + jnp.concatenate([bias_ref[2 * g + e, deltas[j]] for e in range(2)], axis=0)
         for n, (j, g) in enumerate(tasks)]
    pr = [jnp.exp2(sn - jnp.max(sn, axis=-1, keepdims=True)) for sn in s]
    o = [_dot(pr[n].astype(BF16), vw[n]) * (1.0 / jnp.sum(pr[n], axis=-1, keepdims=True))
         for n in range(len(tasks))]
    for n, (j, g) in enumerate(tasks):
        out_ref[0, j * GRID_W:(j + 1) * GRID_W, lanes(g)] = jnp.where(
            head_lanes[0], o[n][:GRID_W], o[n][GRID_W:]).astype(out_ref.dtype)


def _nat_bias_table(rpb):
    c = np.arange(GRID_W)[:, None]
    cp = np.arange(GRID_W)[None, :]
    cs = np.clip(c - WIN_W // 2, 0, GRID_W - WIN_W)
    valid = (cp >= cs) & (cp < cs + WIN_W)
    n_dj = 2 * WIN_W - 1
    n_di = 2 * WIN_H - 1
    onehot = ((cp - c + (WIN_W - 1))[None] == np.arange(n_dj)[:, None, None]) & valid[None]
    toe = jnp.einsum("hdj,jcx->hcdx", rpb.astype(F32), jnp.asarray(onehot, F32),
                     precision=lax.Precision.HIGHEST)
    toe = toe * LOG2_E + jnp.asarray(np.where(valid, 0.0, NEG_BIG), F32)[None, :, None, :]
    toe = toe.reshape(N_NAT_HEADS, GRID_W, n_di * GRID_W)
    tab = jnp.stack([toe[:, :, (WIN_H - 1 - dl) * GRID_W:(WIN_H - 1 - dl) * GRID_W + WIN_TOKENS]
                     for dl in range(WIN_H)], axis=1)
    return tab


def _nat_call(p_nat, bias_tab):
    b, t, _ = p_nat.shape
    n_rows = t // GRID_W
    rb = math.gcd(n_rows, NAT_ROWS)
    return pl.pallas_call(
        functools.partial(_nat_kernel, n_rows, rb),
        out_shape=jax.ShapeDtypeStruct((b, t, D_NAT), BF16),
        grid=(b, n_rows // rb),
        in_specs=[pl.BlockSpec((1, rb * GRID_W, D_NAT), lambda bi, i: (bi, i, 0)),
                  pl.BlockSpec((1, t, D_NAT), lambda bi, i: (bi, 0, 1)),
                  pl.BlockSpec((1, t, D_NAT), lambda bi, i: (bi, 0, 2)),
                  _full(bias_tab.shape)],
        out_specs=pl.BlockSpec((1, rb * GRID_W, D_NAT), lambda bi, i: (bi, i, 0)),
        compiler_params=_params(("parallel", "arbitrary")),
        name="nat",
    )(p_nat, p_nat, p_nat, bias_tab)


def _merge_xattn_kernel(batch_starts, *refs):
    n_src = len(batch_starts)
    x_refs = refs[:n_src]
    (yr_ref, yn_ref, gate_ref, kv_ref, wbr_ref, wbn_ref, wout_ref,
     gx_ref, wq_ref, wo_ref, out_ref) = refs[n_src:]
    bi = pl.program_id(0)
    x_in = x_refs[0][0]
    for s in range(1, n_src):
        x_in = jnp.where(bi >= batch_starts[s], x_refs[s][0], x_in)
    gates = gate_ref[0].astype(F32)
    mixed = (_sigmoid(gates[:, :D_MODEL]) * _dot(yr_ref[0], wbr_ref[...])
             + _sigmoid(gates[:, D_MODEL:]) * _dot(yn_ref[0], wbn_ref[...]))
    x1 = x_in + _dot(mixed.astype(BF16), wout_ref[...])
    q = _dot(_rmsnorm(x1, gx_ref[...]).astype(BF16), wq_ref[...])
    scale = XATTN_HEAD_DIM ** -0.5
    heads = []
    for h in range(N_XATTN_HEADS):
        sl = slice(h * XATTN_HEAD_DIM, (h + 1) * XATTN_HEAD_DIM)
        qh = (q[:, sl] * scale).astype(BF16)
        kh = kv_ref[0, :, sl]
        vh = kv_ref[0, :, D_MODEL + h * XATTN_HEAD_DIM:D_MODEL + (h + 1) * XATTN_HEAD_DIM]
        s = _dot_nt(qh, kh)
        m = jnp.max(s, axis=-1, keepdims=True)
        pr = jnp.exp(s - m)
        inv_l = 1.0 / jnp.sum(pr, axis=-1, keepdims=True)
        heads.append(_dot(pr.astype(BF16), vh) * inv_l)
    o = jnp.concatenate(heads, axis=1).astype(BF16)
    out_ref[0] = x1 + _dot(o, wo_ref[...])


def _merge_xattn_call(xs, y_r, y_n, gates, kv, wbr, wbn, wout, gx, wq, wo, tm):
    b, t, _ = y_r.shape
    d = xs[0].shape[2]
    n_mem = kv.shape[1]
    tile = lambda w: pl.BlockSpec((1, tm, w), lambda bi, i: (bi, i, 0))
    ws = [wbr, wbn, wout, gx.reshape(1, d), wq, wo]
    starts = tuple(int(v) for v in np.cumsum([0] + [x.shape[0] for x in xs[:-1]]))
    x_specs = [pl.BlockSpec((1, tm, d), lambda bi, i, st=st, nb=x.shape[0]:
                            (jnp.clip(bi - st, 0, nb - 1), i, 0)) for x, st in zip(xs, starts)]
    return pl.pallas_call(
        functools.partial(_merge_xattn_kernel, starts),
        out_shape=jax.ShapeDtypeStruct((b, t, d), F32),
        grid=(b, t // tm),
        in_specs=x_specs + [tile(D_RWKV), tile(D_NAT), tile(GATE_COLS),
                            pl.BlockSpec((1, n_mem, 2 * d), lambda bi, i: (bi, 0, 0))]
        + [_full(w.shape) for w in ws],
        out_specs=tile(d),
        compiler_params=_params(("parallel", "arbitrary")),
        name="merge_xattn",
    )(*xs, y_r, y_n, gates, kv, *ws)


def _ffn_kernel(final, ff_chunk, tile_starts, x_ref, g_ref, w1_ref, w2_ref, gf_ref, *out_refs):
    x = x_ref[...]
    h = _rmsnorm(x, g_ref[...]).astype(BF16)
    acc = x
    for j in range(D_FF // ff_chunk):
        sl = slice(j * ff_chunk, (j + 1) * ff_chunk)
        hf = jnp.maximum(_dot(h, w1_ref[:, sl]), 0.0)
        acc = acc + _dot((hf * hf).astype(BF16), w2_ref[sl, :])
    if final:
        acc = _rmsnorm(acc, gf_ref[...])
    if len(out_refs) == 1:
        out_refs[0][...] = acc
        return
    i = pl.program_id(0)
    bounds = list(tile_starts) + [None]
    for k, out_ref in enumerate(out_refs):
        lo, hi = bounds[k], bounds[k + 1]
        cond = (i >= lo) if hi is None else ((i >= lo) & (i < hi))

        @pl.when(cond)
        def _(out_ref=out_ref):
            out_ref[...] = acc


def _ffn_call(x2d, g, w1, w2, g_final, final, tm, out_rows):
    n, d = x2d.shape
    starts = _tile_starts(out_rows, tm)
    out_specs = [pl.BlockSpec((tm, d), lambda i, st=st, nt=r // tm: (jnp.clip(i - st, 0, nt - 1), 0))
                 for r, st in zip(out_rows, starts)]
    return pl.pallas_call(
        functools.partial(_ffn_kernel, final, 1024, starts),
        out_shape=[jax.ShapeDtypeStruct((r, d), F32) for r in out_rows],
        grid=(n // tm,),
        in_specs=[pl.BlockSpec((tm, d), lambda i: (i, 0)), _full((1, d)),
                  _full(w1.shape), _full(w2.shape), _full((1, d))],
        out_specs=out_specs,
        compiler_params=_params(("arbitrary",)),
        name="ffn",
    )(x2d, g.reshape(1, d), w1, w2, g_final.reshape(1, d))


def _pad_dir(w, d):
    z = jnp.zeros_like(w[d])
    parts = [w[0], z] if d == 0 else [z, w[1]]
    return jnp.concatenate(parts, axis=0).astype(BF16)


def _trunk(xs, mem, norm_mix, w_in, mu_prev, mu_next, w0, w_up, a0, a_up, g_up, k_k, k_a, r_k,
           gn_g, gn_b, rpb, w_br_rwkv, w_br_nat, w_out, norm_x, norm_mem, w_xq, w_xkv, w_xo,
           norm_ff, w_ff1, w_ff2, norm_final, tm=512):
    t, d = xs[0].shape[1:]
    assert d == D_MODEL and t % tm == 0 and t % (NAT_ROWS * GRID_W) == 0 and t >= WIN_TOKENS
    batches = [x.shape[0] for x in xs]
    b = sum(batches)
    n_mem = mem.shape[1]
    depth = w_in.shape[0]
    head_id = np.arange(2 * LANES) // HEAD_DIM
    seg = jnp.asarray(head_id[:, None] == head_id[None, :], dtype=BF16)
    row = lambda a: a.reshape(1, -1).astype(F32)
    mem2d = mem.reshape(b * n_mem, d)
    for l in range(depth):
        col_scale = np.ones((1, w_in.shape[2]), np.float32)
        col_scale[:, RWKV_COLS:RWKV_COLS + D_NAT] = HEAD_DIM ** -0.5 * LOG2_E
        w_in_l = (w_in[l] * col_scale).astype(BF16)
        p_rw, p_nat, p_gate = _in_proj(
            [x.reshape(-1, d) for x in xs], norm_mix[l], row(mu_prev[l]), row(mu_next[l]),
            w_in_l[:, :RWKV_COLS], w_in_l[:, RWKV_COLS:RWKV_COLS + NAT_COLS],
            w_in_l[:, RWKV_COLS + NAT_COLS:], t, tm)
        p_rw = p_rw.reshape(b, t, RWKV_COLS)
        p_nat = p_nat.reshape(b, t, NAT_COLS)
        p_gate = p_gate.reshape(b, t, GATE_COLS)

        shared = [row(k_k[l]), row(k_a[l]), seg]
        dir_params = lambda dd: [row(0.5 * w0[l, dd]), _pad_dir(0.5 * w_up[l], dd),
                                 row(0.5 * a0[l, dd]), _pad_dir(0.5 * a_up[l], dd)]
        y_f = _wkv_call(False, p_rw, shared, dir_params(0), None)
        extra = [y_f, row(0.5 * a0[l, 0]), _pad_dir(0.5 * a_up[l], 0), g_up[l].astype(BF16), row(r_k[l]),
                 row(gn_g[l]), row(gn_b[l])]
        y_r = _wkv_call(True, p_rw, shared, dir_params(1), extra)

        y_n = _nat_call(p_nat, _nat_bias_table(rpb[l]))

        (kv,) = _norm_proj(mem2d, norm_mem[l], [w_xkv[l].astype(BF16)], [BF16], tm, "mem_kv")
        kv = kv.reshape(b, n_mem, 2 * d)
        x = _merge_xattn_call(xs, y_r, y_n, p_gate, kv, w_br_rwkv[l].astype(BF16),
                              w_br_nat[l].astype(BF16), w_out[l].astype(BF16), norm_x[l],
                              w_xq[l].astype(BF16), w_xo[l].astype(BF16), tm)
        last = l == depth - 1
        outs = _ffn_call(x.reshape(b * t, d), norm_ff[l], w_ff1[l].astype(BF16),
                         w_ff2[l].astype(BF16), norm_final, last, tm,
                         [bk * t for bk in batches] if last else [b * t])
        xs = [o.reshape(-1, t, d) for o in outs]
    return xs


def kernel(x_prompt, x_sample, mem_prompt, mem_sample, norm_mix, w_in, mu_prev, mu_next, w0, w_up, a0, a_up, g_up, k_k, k_a, r_k, gn_g, gn_b, rpb, w_br_rwkv, w_br_nat, w_out, norm_x, norm_mem, w_xq, w_xkv, w_xo, norm_ff, w_ff1, w_ff2, norm_final):
    assert x_prompt.shape[1:] == x_sample.shape[1:] and mem_prompt.shape[1:] == mem_sample.shape[1:]
    mem = jnp.concatenate([mem_prompt, mem_sample], axis=0)
    y_prompt, y_sample = _trunk(
        [x_prompt, x_sample], mem, norm_mix, w_in, mu_prev, mu_next, w0, w_up, a0, a_up, g_up, k_k,
        k_a, r_k, gn_g, gn_b, rpb, w_br_rwkv, w_br_nat, w_out, norm_x, norm_mem, w_xq, w_xkv, w_xo,
        norm_ff, w_ff1, w_ff2, norm_final)
    return (y_prompt, y_sample)
```

```python
import functools
import math

import numpy as np
import jax
import jax.numpy as jnp
from jax import lax
from jax.experimental import pallas as pl
from jax.experimental.pallas import tpu as pltpu

F32 = jnp.float32
BF16 = jnp.bfloat16

D_MODEL = 1024
GRID_W = 64
HEAD_DIM = 64
D_RWKV = 512
D_NAT = 512
N_NAT_HEADS = D_NAT // HEAD_DIM
LORA_W = 64
LORA_A = 64
LORA_G = 128
WIN_H = 8
WIN_W = 16
N_XATTN_HEADS = 4
XATTN_HEAD_DIM = D_MODEL // N_XATTN_HEADS
D_FF = 4 * D_MODEL
NORM_EPS = 1e-6
GN_EPS = 1e-5 * HEAD_DIM
RWKV_COLS = 3 * D_RWKV + 2 * LORA_W + 2 * LORA_A + LORA_G
NAT_COLS = 3 * D_NAT
GATE_COLS = 2 * D_MODEL
COL_WD = 3 * D_RWKV
COL_AD = COL_WD + 2 * LORA_W
COL_GD = COL_AD + 2 * LORA_A
COL_KKN = RWKV_COLS
RWKV_OUT_COLS = RWKV_COLS + D_RWKV

LANES = 128
N_PAIRS = D_RWKV // LANES
CHUNK = 64
ROW_TILE = 16
WKV_SEQS = 6
NAT_ROWS = 4
WIN_TOKENS = WIN_H * GRID_W
NEG_BIG = -1e30
LOG2_E = math.log2(math.e)
VMEM_LIMIT = 56 * 1024 * 1024


def _dot(a, b):
    return jnp.dot(a, b, preferred_element_type=F32)


def _dot_nt(a, b):
    return lax.dot_general(a, b, (((1,), (1,)), ((), ())), preferred_element_type=F32)


def _dot_tn(a, b):
    return lax.dot_general(a, b, (((0,), (0,)), ((), ())), preferred_element_type=F32)


def _dot_split_lhs(a, b):
    hi = b.astype(BF16)
    lo = (b - hi.astype(F32)).astype(BF16)
    return _dot(a, hi) + _dot(a, lo)


def _sigmoid_of_half(xh):
    return 0.5 * jnp.tanh(xh) + 0.5


def _sigmoid(x):
    return _sigmoid_of_half(0.5 * x)


def _rmsnorm(x, g):
    return x * lax.rsqrt(jnp.mean(x * x, axis=-1, keepdims=True) + NORM_EPS) * g


def _params(sem):
    return pltpu.CompilerParams(dimension_semantics=sem, vmem_limit_bytes=VMEM_LIMIT)


def _full(shape):
    nd = len(shape)
    return pl.BlockSpec(shape, lambda *_: (0,) * nd, pipeline_mode=pl.Buffered(1))


def _norm_proj_kernel(n_out, x_ref, g_ref, *refs):
    w_refs, o_refs = refs[:n_out], refs[n_out:]
    h = _rmsnorm(x_ref[...], g_ref[...]).astype(BF16)
    for w_ref, o_ref in zip(w_refs, o_refs):
        o_ref[...] = _dot(h, w_ref[...]).astype(o_ref.dtype)


def _norm_proj(x2d, g, ws, out_dtypes, tm, name):
    n, d = x2d.shape
    tm = math.gcd(n, tm)
    n_out = len(ws)
    return pl.pallas_call(
        functools.partial(_norm_proj_kernel, n_out),
        out_shape=[jax.ShapeDtypeStruct((n, w.shape[1]), dt) for w, dt in zip(ws, out_dtypes)],
        grid=(n // tm,),
        in_specs=[pl.BlockSpec((tm, d), lambda i: (i, 0)), _full((1, d))]
        + [_full(w.shape) for w in ws],
        out_specs=[pl.BlockSpec((tm, w.shape[1]), lambda i: (i, 0)) for w in ws],
        compiler_params=_params(("parallel",)),
        name=name,
    )(x2d, g.reshape(1, d), *ws)


def _seg_sum(x, seg):
    w = seg.shape[0]
    return jnp.concatenate([_dot(x[:, j:j + w].astype(BF16), seg) for j in range(0, x.shape[1], w)],
                           axis=1)


def _tile_starts(row_counts, tm):
    return tuple(int(s) // tm for s in np.cumsum([0] + list(row_counts[:-1])))


def _in_proj_kernel(tiles_per_seq, tile_starts, *refs):
    n_src = len(tile_starts)
    src_refs = refs[:3 * n_src]
    (g_ref, mup_ref, mun_ref, kk_ref, seg_ref, wrw_ref, wnat_ref, wgate_ref,
     prw_ref, pnat_ref, pgate_ref) = refs[3 * n_src:]
    i = pl.program_id(0)

    def pick(k):
        val = src_refs[k][...]
        for s in range(1, n_src):
            val = jnp.where(i >= tile_starts[s], src_refs[3 * s + k][...], val)
        return val

    x = pick(0)
    tm = x.shape[0]
    g = g_ref[...]
    h = _rmsnorm(x, g)
    pos = i % tiles_per_seq
    keep_prev = jnp.where(pos == 0, 0.0, 1.0)
    keep_next = jnp.where(pos == tiles_per_seq - 1, 0.0, 1.0)
    h_ext = jnp.concatenate([_rmsnorm(pick(1), g) * keep_prev, h,
                             _rmsnorm(pick(2), g) * keep_next], axis=0)
    p = _dot(h_ext.astype(BF16), wrw_ref[...])
    p_mid = p[8:8 + tm]
    p_prev = pltpu.roll(p, 1, 0)[8:8 + tm]
    p_next = pltpu.roll(p, tm + 15, 0)[8:8 + tm]
    ps = p_mid + mup_ref[...] * (p_prev - p_mid) + mun_ref[...] * (p_next - p_mid)
    prw_ref[:, :RWKV_COLS] = ps
    kkv = ps[:, D_RWKV:2 * D_RWKV] * kk_ref[...]
    prw_ref[:, COL_KKN:] = kkv * lax.rsqrt(jnp.maximum(_seg_sum(kkv * kkv, seg_ref[...]), 1e-24))
    hb = h.astype(BF16)
    pnat_ref[...] = _dot(hb, wnat_ref[...]).astype(pnat_ref.dtype)
    pgate_ref[...] = _dot(hb, wgate_ref[...]).astype(pgate_ref.dtype)


def _in_proj(xs, g, mup, mun, kk, seg, w_rw, w_nat, w_gate, seq_len, tm):
    d = xs[0].shape[1]
    n = sum(x.shape[0] for x in xs)
    per8 = tm // 8
    ws = [w_rw, w_nat, w_gate]
    starts = _tile_starts([x.shape[0] for x in xs], tm)
    src_specs, src_args = [], []
    for x, st in zip(xs, starts):
        nt = x.shape[0] // tm
        local = lambda i, st=st, nt=nt: jnp.clip(i - st, 0, nt - 1)
        last8 = x.shape[0] // 8 - 1
        src_specs += [
            pl.BlockSpec((tm, d), lambda i, local=local: (local(i), 0)),
            pl.BlockSpec((8, d), lambda i, local=local: (jnp.maximum(local(i) * per8 - 1, 0), 0)),
            pl.BlockSpec((8, d), lambda i, local=local, last8=last8:
                         (jnp.minimum((local(i) + 1) * per8, last8), 0))]
        src_args += [x, x, x]
    return pl.pallas_call(
        functools.partial(_in_proj_kernel, seq_len // tm, starts),
        out_shape=[jax.ShapeDtypeStruct((n, RWKV_OUT_COLS), F32),
                   jax.ShapeDtypeStruct((n, NAT_COLS), BF16),
                   jax.ShapeDtypeStruct((n, GATE_COLS), BF16)],
        grid=(n // tm,),
        in_specs=src_specs + [_full((1, d)), _full(mup.shape), _full(mun.shape), _full(kk.shape),
                              _full(seg.shape)] + [_full(w.shape) for w in ws],
        out_specs=[pl.BlockSpec((tm, c), lambda i: (i, 0))
                   for c in (RWKV_OUT_COLS, NAT_COLS, GATE_COLS)],
        compiler_params=_params(("parallel",)),
        name="in_proj",
    )(*src_args, g.reshape(1, d), mup, mun, kk, seg, *ws)


def _wkv_kernel(reverse, n_chunks, nb, *refs):
    if reverse:
        (p_ref, w0_ref, wup_ref, a0_ref, aup_ref,
         ka_ref, seg_ref, yf_ref, a0o_ref, aupo_ref, gup_ref, rk_ref, gng_ref, gnb_ref,
         out_ref, s_ref) = refs
    else:
        (p_ref, w0_ref, wup_ref, a0_ref, aup_ref,
         ka_ref, seg_ref, out_ref, s_ref) = refs

    c = pl.program_id(1)

    @pl.when(c == 0)
    def _():
        s_ref[...] = jnp.zeros_like(s_ref)

    seg = seg_ref[...]
    ti = lax.broadcasted_iota(jnp.int32, (CHUNK, CHUNK), 0)
    si = lax.broadcasted_iota(jnp.int32, (CHUNK, CHUNK), 1)
    tri = ((si >= ti) if reverse else (si <= ti)).astype(BF16)
    last = 0 if reverse else CHUNK - 1
    ri = lax.broadcasted_iota(jnp.int32, (2 * CHUNK, LANES), 0)
    li = lax.broadcasted_iota(jnp.int32, (2 * CHUNK, LANES), 1)
    ri2 = lax.broadcasted_iota(jnp.int32, (4 * CHUNK, LANES), 0)
    li2 = lax.broadcasted_iota(jnp.int32, (4 * CHUNK, LANES), 1)
    t_idx = ri2 % CHUNK
    s_idx = li2 % CHUNK
    incl = (ri2 // CHUNK) % 2
    sc_mask2 = (s_idx > t_idx - incl) if reverse else (s_idx < t_idx + incl)
    lane = lax.broadcasted_iota(jnp.int32, (1, LANES), 1)
    head_lanes = (lane < HEAD_DIM, lane >= HEAD_DIM)
    bd_mask = (ri // HEAD_DIM) == (li // HEAD_DIM)
    zeros_cv = jnp.zeros((CHUNK, LANES), BF16)
    n_steps = int(math.log2(CHUNK))

    def prepare(seqs):
        ps = jnp.concatenate([p_ref[i] for i in seqs], axis=0)
        r = ps[:, 0:D_RWKV]
        k = ps[:, D_RWKV:2 * D_RWKV]
        v = ps[:, 2 * D_RWKV:3 * D_RWKV]
        wd = ps[:, COL_WD:COL_AD]
        ad = ps[:, COL_AD:COL_GD]
        half_w = w0_ref[...] + _dot(jnp.tanh(wd).astype(BF16), wup_ref[...])
        half_c = -0.5 * math.exp(-0.5) * LOG2_E
        logw = half_c * jnp.tanh(half_w) + half_c
        ad_b = ad.astype(BF16)
        neg_a = -0.5 * jnp.tanh(a0_ref[...] + _dot(ad_b, aup_ref[...])) - 0.5
        kkn = ps[:, COL_KKN:RWKV_OUT_COLS]
        ka = ka_ref[...]
        one_m_ka, neg_ka = 1.0 - ka, -ka
        k_d = k * (one_m_ka + neg_a * neg_ka)
        b_vec = kkn * neg_a
        cums = [_dot_split_lhs(tri, logw[n * CHUNK:(n + 1) * CHUNK]) for n in range(len(seqs))]
        p_tots = [jnp.exp2(cm[last:last + 1]) for cm in cums]
        cum = jnp.concatenate(cums, axis=0) if len(seqs) > 1 else cums[0]
        a_t = kkn * jnp.exp2(cum - logw)
        r_t = r * jnp.exp2(cum)
        e_inv = jnp.exp2(-cum)
        prep = dict(a_t=a_t, r_t=r_t, b_t=b_vec * e_inv, k_t=k_d * e_inv, v=v, p_tots=p_tots)
        if reverse:
            neg_ao = -0.5 * jnp.tanh(a0o_ref[...] + _dot(ad_b, aupo_ref[...])) - 0.5
            k_sum = k_d + k * (one_m_ka + neg_ao * neg_ka)
            prep["bonus_v"] = _seg_sum(r * k_sum * rk_ref[...], seg) * v
            prep["gate"] = _dot(_sigmoid(ps[:, COL_GD:RWKV_COLS]).astype(BF16), gup_ref[...])
        return prep

    def scan(seqs, prep):
        tasks = [(n, i, g) for n, i in enumerate(seqs) for g in range(N_PAIRS)]

        def tile(a, n, g):
            return a[n * CHUNK:(n + 1) * CHUNK, g * LANES:(g + 1) * LANES]

        lhs_f = [jnp.concatenate([tile(prep["a_t"], n, g), tile(prep["r_t"], n, g)], axis=0)
                 for n, _, g in tasks]
        rhs_f = [jnp.concatenate([tile(prep["b_t"], n, g), tile(prep["k_t"], n, g)], axis=0)
                 for n, _, g in tasks]
        rhs = [rf.astype(BF16) for rf in rhs_f]
        p_tot = [prep["p_tots"][n][:, g * LANES:(g + 1) * LANES] for n, _, g in tasks]
        bk_o = [(rf * pt).astype(BF16) for rf, pt in zip(rhs_f, p_tot)]
        v_g = [tile(prep["v"], n, g).astype(BF16) for n, _, g in tasks]
        state = [s_ref[i, g] for _, i, g in tasks]
        from_state = [_dot_nt(lf.astype(BF16), st.astype(BF16)) for lf, st in zip(lhs_f, state)]
        sc2 = [jnp.where(sc_mask2, _dot_nt(jnp.concatenate(
            [jnp.where(head_lanes[e], lf, 0.0) for e in range(2)], axis=0).astype(BF16), rh), 0.0)
               for lf, rh in zip(lhs_f, rhs)]
        top = [[s2[2 * CHUNK * e:2 * CHUNK * e + CHUNK] for e in range(2)] for s2 in sc2]
        bot = [[s2[2 * CHUNK * e + CHUNK:2 * CHUNK * (e + 1)] for e in range(2)] for s2 in sc2]
        zv = [jnp.concatenate([zeros_cv, vg], axis=0) for vg in v_g]
        akv = [_dot(jnp.concatenate(tp, axis=0).astype(BF16), z) for tp, z in zip(top, zv)]
        x0 = [fs[:CHUNK] + jnp.where(head_lanes[0], ak[:CHUNK], ak[CHUNK:])
              for fs, ak in zip(from_state, akv)]

        a_both = [jnp.where(head_lanes[0], tp[0], pltpu.roll(tp[1], HEAD_DIM, 1)) for tp in top]
        xs = list(x0)
        lane2 = lax.broadcasted_iota(jnp.int32, (1, 2 * LANES), 1) % LANES
        own2 = (lane2 < HEAD_DIM, lane2 >= HEAD_DIM)
        for it in range(n_steps):
            skip = ((1 << it) // ROW_TILE) * ROW_TILE
            lo, hi = (0, CHUNK - skip) if reverse else (skip, CHUNK)
            pad_rows = lambda m: jnp.concatenate(
                ([jnp.zeros((lo, LANES), F32)] if lo else []) + [m]
                + ([jnp.zeros((CHUNK - hi, LANES), F32)] if hi < CHUNK else []), axis=0)
            add_rows = lambda x, dx: jnp.concatenate(
                ([x[:lo]] if lo else []) + [x[lo:hi] + dx] + ([x[hi:]] if hi < CHUNK else []),
                axis=0)
            squaring = it < n_steps - 1
            own = own2 if squaring else head_lanes
            for n in range(len(tasks)):
                w = jnp.concatenate([a_both[n], xs[n]], axis=1) if squaring else xs[n]
                z = jnp.concatenate([jnp.where(own[e], w, 0.0) for e in range(2)],
                                    axis=0).astype(BF16)
                rz = _dot(a_both[n][lo:hi].astype(BF16), z)
                if squaring:
                    a_both[n] = pad_rows(rz[:, :LANES])
                    xs[n] = add_rows(xs[n], rz[:, LANES:])
                else:
                    xs[n] = add_rows(xs[n], rz)

        ys = [[None] * N_PAIRS for _ in seqs]
        for t, (n, i, g) in enumerate(tasks):
            uv = jnp.concatenate([xs[t].astype(BF16), v_g[t]], axis=0)
            rbk = _dot(jnp.concatenate(bot[t], axis=0).astype(BF16), uv)
            ys[n][g] = from_state[t][CHUNK:] + jnp.where(head_lanes[0], rbk[:CHUNK], rbk[CHUNK:])
            upd = _dot_tn(uv, bk_o[t])
            s_ref[i, g] = state[t] * p_tot[t] + jnp.where(bd_mask, upd, 0.0)
        y = jnp.concatenate([jnp.concatenate(yn, axis=1) for yn in ys], axis=0)
        if reverse:
            out = jnp.concatenate([yf_ref[i] for i in seqs], axis=0) + y
            inv_n = 1.0 / HEAD_DIM
            mean = _seg_sum(out, seg) * inv_n
            cen = out - mean
            var = _seg_sum(cen * cen, seg) * inv_n
            out = cen * lax.rsqrt(var + GN_EPS) * gng_ref[...] + gnb_ref[...]
            y = (out + prep["bonus_v"]) * prep["gate"]
        for n, i in enumerate(seqs):
            out_ref[i] = y[n * CHUNK:(n + 1) * CHUNK].astype(out_ref.dtype)

    seqs = list(range(nb))
    scan(seqs, prepare(seqs))


def _wkv_call(reverse, p_rw, shared, dir_params, extra):
    b, t, _ = p_rw.shape
    n_chunks = t // CHUNK
    nb = math.gcd(b, WKV_SEQS)
    cur = lambda bi, c: (bi, (n_chunks - 1 - c) if reverse else c, 0)
    args = [p_rw] + list(dir_params) + list(shared)
    in_specs = [pl.BlockSpec((nb, CHUNK, RWKV_OUT_COLS), cur)] + [_full(a.shape) for a in args[1:]]
    if reverse:
        y_f = extra[0]
        args += [y_f] + list(extra[1:])
        in_specs += [pl.BlockSpec((nb, CHUNK, D_RWKV), cur)] + [_full(a.shape) for a in extra[1:]]
        out_dtype = BF16
    else:
        out_dtype = F32
    return pl.pallas_call(
        functools.partial(_wkv_kernel, reverse, n_chunks, nb),
        out_shape=jax.ShapeDtypeStruct((b, t, D_RWKV), out_dtype),
        grid=(b // nb, n_chunks),
        in_specs=in_specs,
        out_specs=pl.BlockSpec((nb, CHUNK, D_RWKV), cur),
        scratch_shapes=[pltpu.VMEM((nb, N_PAIRS, LANES, LANES), F32)],
        compiler_params=_params(("parallel", "arbitrary")),
        name="wkv_bwd" if reverse else "wkv_fwd",
    )(*args)


def _nat_kernel(n_rows, rb, q_ref, k_ref, v_ref, bias_ref, out_ref):
    i0 = pl.program_id(1) * rb
    lane = lax.broadcasted_iota(jnp.int32, (1, LANES), 1)
    head_lanes = (lane < HEAD_DIM, lane >= HEAD_DIM)
    starts, deltas = [], []
    for j in range(rb):
        rs = jnp.clip(i0 + j - WIN_H // 2, 0, n_rows - WIN_H)
        starts.append(pl.multiple_of(rs * GRID_W, GRID_W))
        deltas.append(i0 + j - rs)
    tasks = [(j, g) for j in range(rb) for g in range(N_PAIRS)]
    lanes = lambda g: slice(g * LANES, (g + 1) * LANES)
    kw = [k_ref[0, pl.ds(starts[j], WIN_TOKENS), lanes(g)] for j, g in tasks]
    vw = [v_ref[0, pl.ds(starts[j], WIN_TOKENS), lanes(g)] for j, g in tasks]
    qg = [q_ref[0, j * GRID_W:(j + 1) * GRID_W, lanes(g)].astype(F32) for j, g in tasks]
    s = [_dot_nt(jnp.concatenate([jnp.where(head_lanes[e], qg[n], 0.0) for e in range(2)],
                                 axis=0).astype(BF16), kw[n])
         + jnp.concatenate([bias_ref[2 * g + e, deltas[j]] for e in range(2)], axis=0)
         for n, (j, g) in enumerate(tasks)]
    pr = [jnp.exp2(sn - jnp.max(sn, axis=-1, keepdims=True)) for sn in s]
    o = [_dot(pr[n].astype(BF16), vw[n]) * (1.0 / jnp.sum(pr[n], axis=-1, keepdims=True))
         for n in range(len(tasks))]
    for n, (j, g) in enumerate(tasks):
        out_ref[0, j * GRID_W:(j + 1) * GRID_W, lanes(g)] = jnp.where(
            head_lanes[0], o[n][:GRID_W], o[n][GRID_W:]).astype(out_ref.dtype)


def _nat_bias_table(rpb):
    c = np.arange(GRID_W)[:, None]
    cp = np.arange(GRID_W)[None, :]
    cs = np.clip(c - WIN_W // 2, 0, GRID_W - WIN_W)
    valid = (cp >= cs) & (cp < cs + WIN_W)
    n_dj = 2 * WIN_W - 1
    n_di = 2 * WIN_H - 1
    onehot = ((cp - c + (WIN_W - 1))[None] == np.arange(n_dj)[:, None, None]) & valid[None]
    toe = jnp.einsum("hdj,jcx->hcdx", rpb.astype(F32), jnp.asarray(onehot, F32),
                     precision=lax.Precision.HIGHEST)
    toe = toe * LOG2_E + jnp.asarray(np.where(valid, 0.0, NEG_BIG), F32)[None, :, None, :]
    toe = toe.reshape(N_NAT_HEADS, GRID_W, n_di * GRID_W)
    tab = jnp.stack([toe[:, :, (WIN_H - 1 - dl) * GRID_W:(WIN_H - 1 - dl) * GRID_W + WIN_TOKENS]
                     for dl in range(WIN_H)], axis=1)
    return tab


def _nat_call(p_nat, bias_tab):
    b, t, _ = p_nat.shape
    n_rows = t // GRID_W
    rb = math.gcd(n_rows, NAT_ROWS)
    return pl.pallas_call(
        functools.partial(_nat_kernel, n_rows, rb),
        out_shape=jax.ShapeDtypeStruct((b, t, D_NAT), BF16),
        grid=(b, n_rows // rb),
        in_specs=[pl.BlockSpec((1, rb * GRID_W, D_NAT), lambda bi, i: (bi, i, 0)),
                  pl.BlockSpec((1, t, D_NAT), lambda bi, i: (bi, 0, 1)),
                  pl.BlockSpec((1, t, D_NAT), lambda bi, i: (bi, 0, 2)),
                  _full(bias_tab.shape)],
        out_specs=pl.BlockSpec((1, rb * GRID_W, D_NAT), lambda bi, i: (bi, i, 0)),
        compiler_params=_params(("parallel", "arbitrary")),
        name="nat",
    )(p_nat, p_nat, p_nat, bias_tab)


def _merge_xattn_kernel(batch_starts, *refs):
    n_src = len(batch_starts)
    x_refs = refs[:n_src]
    (yr_ref, yn_ref, gate_ref, kv_ref, wbr_ref, wbn_ref, wout_ref,
     gx_ref, wq_ref, wo_ref, out_ref) = refs[n_src:]
    bi = pl.program_id(0)
    x_in = x_refs[0][0]
    for s in range(1, n_src):
        x_in = jnp.where(bi >= batch_starts[s], x_refs[s][0], x_in)
    gates = gate_ref[0].astype(F32)
    mixed = (_sigmoid(gates[:, :D_MODEL]) * _dot(yr_ref[0], wbr_ref[...])
             + _sigmoid(gates[:, D_MODEL:]) * _dot(yn_ref[0], wbn_ref[...]))
    x1 = x_in + _dot(mixed.astype(BF16), wout_ref[...])
    q = _dot(_rmsnorm(x1, gx_ref[...]).astype(BF16), wq_ref[...])
    scale = XATTN_HEAD_DIM ** -0.5
    heads = []
    for h in range(N_XATTN_HEADS):
        sl = slice(h * XATTN_HEAD_DIM, (h + 1) * XATTN_HEAD_DIM)
        qh = (q[:, sl] * scale).astype(BF16)
        kh = kv_ref[0, :, sl]
        vh = kv_ref[0, :, D_MODEL + h * XATTN_HEAD_DIM:D_MODEL + (h + 1) * XATTN_HEAD_DIM]
        s = _dot_nt(qh, kh)
        m = jnp.max(s, axis=-1, keepdims=True)
        pr = jnp.exp(s - m)
        inv_l = 1.0 / jnp.sum(pr, axis=-1, keepdims=True)
        heads.append(_dot(pr.astype(BF16), vh) * inv_l)
    o = jnp.concatenate(heads, axis=1).astype(BF16)
    out_ref[0] = x1 + _dot(o, wo_ref[...])


def _merge_xattn_call(xs, y_r, y_n, gates, kv, wbr, wbn, wout, gx, wq, wo, tm):
    b, t, _ = y_r.shape
    d = xs[0].shape[2]
    n_mem = kv.shape[1]
    tile = lambda w: pl.BlockSpec((1, tm, w), lambda bi, i: (bi, i, 0))
    ws = [wbr, wbn, wout, gx.reshape(1, d), wq, wo]
    starts = tuple(int(v) for v in np.cumsum([0] + [x.shape[0] for x in xs[:-1]]))
    x_specs = [pl.BlockSpec((1, tm, d), lambda bi, i, st=st, nb=x.shape[0]:
                            (jnp.clip(bi - st, 0, nb - 1), i, 0)) for x, st in zip(xs, starts)]
    return pl.pallas_call(
        functools.partial(_merge_xattn_kernel, starts),
        out_shape=jax.ShapeDtypeStruct((b, t, d), F32),
        grid=(b, t // tm),
        in_specs=x_specs + [tile(D_RWKV), tile(D_NAT), tile(GATE_COLS),
                            pl.BlockSpec((1, n_mem, 2 * d), lambda bi, i: (bi, 0, 0))]
        + [_full(w.shape) for w in ws],
        out_specs=tile(d),
        compiler_params=_params(("parallel", "arbitrary")),
        name="merge_xattn",
    )(*xs, y_r, y_n, gates, kv, *ws)


def _ffn_kernel(final, ff_chunk, tile_starts, x_ref, g_ref, w1_ref, w2_ref, gf_ref, *out_refs):
    x = x_ref[...]
    h = _rmsnorm(x, g_ref[...]).astype(BF16)
    acc = x
    for j in range(D_FF // ff_chunk):
        sl = slice(j * ff_chunk, (j + 1) * ff_chunk)
        hf = jnp.maximum(_dot(h, w1_ref[:, sl]), 0.0)
        acc = acc + _dot((hf * hf).astype(BF16), w2_ref[sl, :])
    if final:
        acc = _rmsnorm(acc, gf_ref[...])
    if len(out_refs) == 1:
        out_refs[0][...] = acc
        return
    i = pl.program_id(0)
    bounds = list(tile_starts) + [None]
    for k, out_ref in enumerate(out_refs):
        lo, hi = bounds[k], bounds[k + 1]
        cond = (i >= lo) if hi is None else ((i >= lo) & (i < hi))

        @pl.when(cond)
        def _(out_ref=out_ref):
            out_ref[...] = acc


def _ffn_call(x2d, g, w1, w2, g_final, final, tm, out_rows):
    n, d = x2d.shape
    starts = _tile_starts(out_rows, tm)
    out_specs = [pl.BlockSpec((tm, d), lambda i, st=st, nt=r // tm: (jnp.clip(i - st, 0, nt - 1), 0))
                 for r, st in zip(out_rows, starts)]
    return pl.pallas_call(
        functools.partial(_ffn_kernel, final, 1024, starts),
        out_shape=[jax.ShapeDtypeStruct((r, d), F32) for r in out_rows],
        grid=(n // tm,),
        in_specs=[pl.BlockSpec((tm, d), lambda i: (i, 0)), _full((1, d)),
                  _full(w1.shape), _full(w2.shape), _full((1, d))],
        out_specs=out_specs,
        compiler_params=_params(("arbitrary",)),
        name="ffn",
    )(x2d, g.reshape(1, d), w1, w2, g_final.reshape(1, d))


def _pad_dir(w, d):
    z = jnp.zeros_like(w[d])
    parts = [w[0], z] if d == 0 else [z, w[1]]
    return jnp.concatenate(parts, axis=0).astype(BF16)


def _trunk(xs, mem, norm_mix, w_in, mu_prev, mu_next, w0, w_up, a0, a_up, g_up, k_k, k_a, r_k,
           gn_g, gn_b, rpb, w_br_rwkv, w_br_nat, w_out, norm_x, norm_mem, w_xq, w_xkv, w_xo,
           norm_ff, w_ff1, w_ff2, norm_final, tm=512):
    t, d = xs[0].shape[1:]
    assert d == D_MODEL and t % tm == 0 and t % (NAT_ROWS * GRID_W) == 0 and t >= WIN_TOKENS
    batches = [x.shape[0] for x in xs]
    b = sum(batches)
    n_mem = mem.shape[1]
    depth = w_in.shape[0]
    head_id = np.arange(2 * LANES) // HEAD_DIM
    seg = jnp.asarray(head_id[:, None] == head_id[None, :], dtype=BF16)
    row = lambda a: a.reshape(1, -1).astype(F32)
    mem2d = mem.reshape(b * n_mem, d)
    for l in range(depth):
        col_scale = np.ones((1, w_in.shape[2]), np.float32)
        col_scale[:, RWKV_COLS:RWKV_COLS + D_NAT] = HEAD_DIM ** -0.5 * LOG2_E
        w_in_l = (w_in[l] * col_scale).astype(BF16)
        p_rw, p_nat, p_gate = _in_proj(
            [x.reshape(-1, d) for x in xs], norm_mix[l], row(mu_prev[l]), row(mu_next[l]),
            row(k_k[l]), seg, w_in_l[:, :RWKV_COLS], w_in_l[:, RWKV_COLS:RWKV_COLS + NAT_COLS],
            w_in_l[:, RWKV_COLS + NAT_COLS:], t, tm)
        p_rw = p_rw.reshape(b, t, RWKV_OUT_COLS)
        p_nat = p_nat.reshape(b, t, NAT_COLS)
        p_gate = p_gate.reshape(b, t, GATE_COLS)

        shared = [row(k_a[l]), seg]
        dir_params = lambda dd: [row(0.5 * w0[l, dd]), _pad_dir(0.5 * w_up[l], dd),
                                 row(0.5 * a0[l, dd]), _pad_dir(0.5 * a_up[l], dd)]
        y_f = _wkv_call(False, p_rw, shared, dir_params(0), None)
        extra = [y_f, row(0.5 * a0[l, 0]), _pad_dir(0.5 * a_up[l], 0), g_up[l].astype(BF16), row(r_k[l]),
                 row(gn_g[l]), row(gn_b[l])]
        y_r = _wkv_call(True, p_rw, shared, dir_params(1), extra)

        y_n = _nat_call(p_nat, _nat_bias_table(rpb[l]))

        (kv,) = _norm_proj(mem2d, norm_mem[l], [w_xkv[l].astype(BF16)], [BF16], tm, "mem_kv")
        kv = kv.reshape(b, n_mem, 2 * d)
        x = _merge_xattn_call(xs, y_r, y_n, p_gate, kv, w_br_rwkv[l].astype(BF16),
                              w_br_nat[l].astype(BF16), w_out[l].astype(BF16), norm_x[l],
                              w_xq[l].astype(BF16), w_xo[l].astype(BF16), tm)
        last = l == depth - 1
        outs = _ffn_call(x.reshape(b * t, d), norm_ff[l], w_ff1[l].astype(BF16),
                         w_ff2[l].astype(BF16), norm_final, last, tm,
                         [bk * t for bk in batches] if last else [b * t])
        xs = [o.reshape(-1, t, d) for o in outs]
    return xs


def kernel(x_prompt, x_sample, mem_prompt, mem_sample, norm_mix, w_in, mu_prev, mu_next, w0, w_up, a0, a_up, g_up, k_k, k_a, r_k, gn_g, gn_b, rpb, w_br_rwkv, w_br_nat, w_out, norm_x, norm_mem, w_xq, w_xkv, w_xo, norm_ff, w_ff1, w_ff2, norm_final):
    assert x_prompt.shape[1:] == x_sample.shape[1:] and mem_prompt.shape[1:] == mem_sample.shape[1:]
    mem = jnp.concatenate([mem_prompt, mem_sample], axis=0)
    y_prompt, y_sample = _trunk(
        [x_prompt, x_sample], mem, norm_mix, w_in, mu_prev, mu_next, w0, w_up, a0, a_up, g_up, k_k,
        k_a, r_k, gn_g, gn_b, rpb, w_br_rwkv, w_br_nat, w_out, norm_x, norm_mem, w_xq, w_xkv, w_xo,
        norm_ff, w_ff1, w_ff2, norm_final)
    return (y_prompt, y_sample)
```

```python
import functools
import math

import numpy as np
import jax
import jax.numpy as jnp
from jax import lax
from jax.experimental import pallas as pl
from jax.experimental.pallas import tpu as pltpu

F32 = jnp.float32
BF16 = jnp.bfloat16

D_MODEL = 1024
GRID_W = 64
HEAD_DIM = 64
D_RWKV = 512
D_NAT = 512
N_NAT_HEADS = D_NAT // HEAD_DIM
LORA_W = 64
LORA_A = 64
LORA_G = 128
WIN_H = 8
WIN_W = 16
N_XATTN_HEADS = 4
XATTN_HEAD_DIM = D_MODEL // N_XATTN_HEADS
D_FF = 4 * D_MODEL
NORM_EPS = 1e-6
GN_EPS = 1e-5 * HEAD_DIM
RWKV_COLS = 3 * D_RWKV + 2 * LORA_W + 2 * LORA_A + LORA_G
NAT_COLS = 3 * D_NAT
GATE_COLS = 2 * D_MODEL
COL_WD = 3 * D_RWKV
COL_AD = COL_WD + 2 * LORA_W
COL_GD = COL_AD + 2 * LORA_A
COL_KKN = RWKV_COLS
RWKV_OUT_COLS = RWKV_COLS + D_RWKV

LANES = 128
N_PAIRS = D_RWKV // LANES
CHUNK = 64
ROW_TILE = 16
WKV_SEQS = 12
NAT_ROWS = 8
WIN_TOKENS = WIN_H * GRID_W
NEG_BIG = -1e30
LOG2_E = math.log2(math.e)
VMEM_LIMIT = 56 * 1024 * 1024


def _dot(a, b):
    return jnp.dot(a, b, preferred_element_type=F32)


def _dot_nt(a, b):
    return lax.dot_general(a, b, (((1,), (1,)), ((), ())), preferred_element_type=F32)


def _dot_tn(a, b):
    return lax.dot_general(a, b, (((0,), (0,)), ((), ())), preferred_element_type=F32)


def _dot_split_lhs(a, b):
    hi = b.astype(BF16)
    lo = (b - hi.astype(F32)).astype(BF16)
    return _dot(a, hi) + _dot(a, lo)


def _sigmoid_of_half(xh):
    return 0.5 * jnp.tanh(xh) + 0.5


def _sigmoid(x):
    return _sigmoid_of_half(0.5 * x)


def _rmsnorm(x, g):
    return x * lax.rsqrt(jnp.mean(x * x, axis=-1, keepdims=True) + NORM_EPS) * g


def _params(sem):
    return pltpu.CompilerParams(dimension_semantics=sem, vmem_limit_bytes=VMEM_LIMIT)


def _full(shape):
    nd = len(shape)
    return pl.BlockSpec(shape, lambda *_: (0,) * nd, pipeline_mode=pl.Buffered(1))


def _norm_proj_kernel(n_out, x_ref, g_ref, *refs):
    w_refs, o_refs = refs[:n_out], refs[n_out:]
    h = _rmsnorm(x_ref[...], g_ref[...]).astype(BF16)
    for w_ref, o_ref in zip(w_refs, o_refs):
        o_ref[...] = _dot(h, w_ref[...]).astype(o_ref.dtype)


def _norm_proj(x2d, g, ws, out_dtypes, tm, name):
    n, d = x2d.shape
    tm = math.gcd(n, tm)
    n_out = len(ws)
    return pl.pallas_call(
        functools.partial(_norm_proj_kernel, n_out),
        out_shape=[jax.ShapeDtypeStruct((n, w.shape[1]), dt) for w, dt in zip(ws, out_dtypes)],
        grid=(n // tm,),
        in_specs=[pl.BlockSpec((tm, d), lambda i: (i, 0)), _full((1, d))]
        + [_full(w.shape) for w in ws],
        out_specs=[pl.BlockSpec((tm, w.shape[1]), lambda i: (i, 0)) for w in ws],
        compiler_params=_params(("parallel",)),
        name=name,
    )(x2d, g.reshape(1, d), *ws)


def _seg_sum(x, seg):
    w = seg.shape[0]
    return jnp.concatenate([_dot(x[:, j:j + w].astype(BF16), seg) for j in range(0, x.shape[1], w)],
                           axis=1)


def _tile_starts(row_counts, tm):
    return tuple(int(s) // tm for s in np.cumsum([0] + list(row_counts[:-1])))


def _in_proj_kernel(tiles_per_seq, tile_starts, *refs):
    n_src = len(tile_starts)
    src_refs = refs[:3 * n_src]
    (g_ref, mup_ref, mun_ref, kk_ref, seg_ref, wrw_ref, wnat_ref, wgate_ref,
     prw_ref, pnat_ref, pgate_ref) = refs[3 * n_src:]
    i = pl.program_id(0)

    def pick(k):
        val = src_refs[k][...]
        for s in range(1, n_src):
            val = jnp.where(i >= tile_starts[s], src_refs[3 * s + k][...], val)
        return val

    x = pick(0)
    tm = x.shape[0]
    g = g_ref[...]
    h = _rmsnorm(x, g)
    pos = i % tiles_per_seq
    keep_prev = jnp.where(pos == 0, 0.0, 1.0)
    keep_next = jnp.where(pos == tiles_per_seq - 1, 0.0, 1.0)
    h_ext = jnp.concatenate([_rmsnorm(pick(1), g) * keep_prev, h,
                             _rmsnorm(pick(2), g) * keep_next], axis=0)
    p = _dot(h_ext.astype(BF16), wrw_ref[...])
    p_mid = p[8:8 + tm]
    p_prev = pltpu.roll(p, 1, 0)[8:8 + tm]
    p_next = pltpu.roll(p, tm + 15, 0)[8:8 + tm]
    ps = p_mid + mup_ref[...] * (p_prev - p_mid) + mun_ref[...] * (p_next - p_mid)
    prw_ref[:, :RWKV_COLS] = ps
    kkv = ps[:, D_RWKV:2 * D_RWKV] * kk_ref[...]
    prw_ref[:, COL_KKN:] = kkv * lax.rsqrt(jnp.maximum(_seg_sum(kkv * kkv, seg_ref[...]), 1e-24))
    hb = h.astype(BF16)
    pnat_ref[...] = _dot(hb, wnat_ref[...]).astype(pnat_ref.dtype)
    pgate_ref[...] = _dot(hb, wgate_ref[...]).astype(pgate_ref.dtype)


def _in_proj(xs, g, mup, mun, kk, seg, w_rw, w_nat, w_gate, seq_len, tm):
    d = xs[0].shape[1]
    n = sum(x.shape[0] for x in xs)
    per8 = tm // 8
    ws = [w_rw, w_nat, w_gate]
    starts = _tile_starts([x.shape[0] for x in xs], tm)
    src_specs, src_args = [], []
    for x, st in zip(xs, starts):
        nt = x.shape[0] // tm
        local = lambda i, st=st, nt=nt: jnp.clip(i - st, 0, nt - 1)
        last8 = x.shape[0] // 8 - 1
        src_specs += [
            pl.BlockSpec((tm, d), lambda i, local=local: (local(i), 0)),
            pl.BlockSpec((8, d), lambda i, local=local: (jnp.maximum(local(i) * per8 - 1, 0), 0)),
            pl.BlockSpec((8, d), lambda i, local=local, last8=last8:
                         (jnp.minimum((local(i) + 1) * per8, last8), 0))]
        src_args += [x, x, x]
    return pl.pallas_call(
        functools.partial(_in_proj_kernel, seq_len // tm, starts),
        out_shape=[jax.ShapeDtypeStruct((n, RWKV_OUT_COLS), F32),
                   jax.ShapeDtypeStruct((n, NAT_COLS), BF16),
                   jax.ShapeDtypeStruct((n, GATE_COLS), BF16)],
        grid=(n // tm,),
        in_specs=src_specs + [_full((1, d)), _full(mup.shape), _full(mun.shape), _full(kk.shape),
                              _full(seg.shape)] + [_full(w.shape) for w in ws],
        out_specs=[pl.BlockSpec((tm, c), lambda i: (i, 0))
                   for c in (RWKV_OUT_COLS, NAT_COLS, GATE_COLS)],
        compiler_params=_params(("parallel",)),
        name="in_proj",
    )(*src_args, g.reshape(1, d), mup, mun, kk, seg, *ws)


def _wkv_kernel(reverse, n_chunks, nb, *refs):
    if reverse:
        (p_ref, w0_ref, wup_ref, a0_ref, aup_ref,
         ka_ref, seg_ref, yf_ref, a0o_ref, aupo_ref, gup_ref, rk_ref, gng_ref, gnb_ref,
         out_ref, s_ref) = refs
    else:
        (p_ref, w0_ref, wup_ref, a0_ref, aup_ref,
         ka_ref, seg_ref, out_ref, s_ref) = refs

    c = pl.program_id(1)

    @pl.when(c == 0)
    def _():
        s_ref[...] = jnp.zeros_like(s_ref)

    seg = seg_ref[...]
    ti = lax.broadcasted_iota(jnp.int32, (CHUNK, CHUNK), 0)
    si = lax.broadcasted_iota(jnp.int32, (CHUNK, CHUNK), 1)
    tri = ((si >= ti) if reverse else (si <= ti)).astype(BF16)
    last = 0 if reverse else CHUNK - 1
    ri = lax.broadcasted_iota(jnp.int32, (2 * CHUNK, LANES), 0)
    li = lax.broadcasted_iota(jnp.int32, (2 * CHUNK, LANES), 1)
    ri2 = lax.broadcasted_iota(jnp.int32, (4 * CHUNK, LANES), 0)
    li2 = lax.broadcasted_iota(jnp.int32, (4 * CHUNK, LANES), 1)
    t_idx = ri2 % CHUNK
    s_idx = li2 % CHUNK
    incl = (ri2 // CHUNK) % 2
    sc_mask2 = (s_idx > t_idx - incl) if reverse else (s_idx < t_idx + incl)
    lane = lax.broadcasted_iota(jnp.int32, (1, LANES), 1)
    head_lanes = (lane < HEAD_DIM, lane >= HEAD_DIM)
    bd_mask = (ri // HEAD_DIM) == (li // HEAD_DIM)
    zeros_cv = jnp.zeros((CHUNK, LANES), BF16)
    n_steps = int(math.log2(CHUNK))

    def prepare(seqs):
        ps = jnp.concatenate([p_ref[i] for i in seqs], axis=0)
        r = ps[:, 0:D_RWKV]
        k = ps[:, D_RWKV:2 * D_RWKV]
        v = ps[:, 2 * D_RWKV:3 * D_RWKV]
        wd = ps[:, COL_WD:COL_AD]
        ad = ps[:, COL_AD:COL_GD]
        half_w = w0_ref[...] + _dot(jnp.tanh(wd).astype(BF16), wup_ref[...])
        half_c = -0.5 * math.exp(-0.5) * LOG2_E
        logw = half_c * jnp.tanh(half_w) + half_c
        ad_b = ad.astype(BF16)
        neg_a = -0.5 * jnp.tanh(a0_ref[...] + _dot(ad_b, aup_ref[...])) - 0.5
        kkn = ps[:, COL_KKN:RWKV_OUT_COLS]
        ka = ka_ref[...]
        one_m_ka, neg_ka = 1.0 - ka, -ka
        k_d = k * (one_m_ka + neg_a * neg_ka)
        b_vec = kkn * neg_a
        cums = [_dot_split_lhs(tri, logw[n * CHUNK:(n + 1) * CHUNK]) for n in range(len(seqs))]
        p_tots = [jnp.exp2(cm[last:last + 1]) for cm in cums]
        cum = jnp.concatenate(cums, axis=0) if len(seqs) > 1 else cums[0]
        a_t = kkn * jnp.exp2(cum - logw)
        r_t = r * jnp.exp2(cum)
        e_inv = jnp.exp2(-cum)
        prep = dict(a_t=a_t, r_t=r_t, b_t=b_vec * e_inv, k_t=k_d * e_inv, v=v, p_tots=p_tots)
        if reverse:
            neg_ao = -0.5 * jnp.tanh(a0o_ref[...] + _dot(ad_b, aupo_ref[...])) - 0.5
            k_sum = k_d + k * (one_m_ka + neg_ao * neg_ka)
            prep["bonus_v"] = _seg_sum(r * k_sum * rk_ref[...], seg) * v
            prep["gate"] = _dot(_sigmoid(ps[:, COL_GD:RWKV_COLS]).astype(BF16), gup_ref[...])
        return prep

    def scan(seqs, prep):
        tasks = [(n, i, g) for n, i in enumerate(seqs) for g in range(N_PAIRS)]

        def tile(a, n, g):
            return a[n * CHUNK:(n + 1) * CHUNK, g * LANES:(g + 1) * LANES]

        lhs_f = [jnp.concatenate([tile(prep["a_t"], n, g), tile(prep["r_t"], n, g)], axis=0)
                 for n, _, g in tasks]
        rhs_f = [jnp.concatenate([tile(prep["b_t"], n, g), tile(prep["k_t"], n, g)], axis=0)
                 for n, _, g in tasks]
        rhs = [rf.astype(BF16) for rf in rhs_f]
        p_tot = [prep["p_tots"][n][:, g * LANES:(g + 1) * LANES] for n, _, g in tasks]
        bk_o = [(rf * pt).astype(BF16) for rf, pt in zip(rhs_f, p_tot)]
        v_g = [tile(prep["v"], n, g).astype(BF16) for n, _, g in tasks]
        state = [s_ref[i, g] for _, i, g in tasks]
        from_state = [_dot_nt(lf.astype(BF16), st.astype(BF16)) for lf, st in zip(lhs_f, state)]
        sc2 = [jnp.where(sc_mask2, _dot_nt(jnp.concatenate(
            [jnp.where(head_lanes[e], lf, 0.0) for e in range(2)], axis=0).astype(BF16), rh), 0.0)
               for lf, rh in zip(lhs_f, rhs)]
        top = [[s2[2 * CHUNK * e:2 * CHUNK * e + CHUNK] for e in range(2)] for s2 in sc2]
        bot = [[s2[2 * CHUNK * e + CHUNK:2 * CHUNK * (e + 1)] for e in range(2)] for s2 in sc2]
        zv = [jnp.concatenate([zeros_cv, vg], axis=0) for vg in v_g]
        akv = [_dot(jnp.concatenate(tp, axis=0).astype(BF16), z) for tp, z in zip(top, zv)]
        x0 = [fs[:CHUNK] + jnp.where(head_lanes[0], ak[:CHUNK], ak[CHUNK:])
              for fs, ak in zip(from_state, akv)]

        a_both = [jnp.where(head_lanes[0], tp[0], pltpu.roll(tp[1], HEAD_DIM, 1)) for tp in top]
        xs = list(x0)
        lane2 = lax.broadcasted_iota(jnp.int32, (1, 2 * LANES), 1) % LANES
        own2 = (lane2 < HEAD_DIM, lane2 >= HEAD_DIM)
        for it in range(n_steps):
            skip = ((1 << it) // ROW_TILE) * ROW_TILE
            lo, hi = (0, CHUNK - skip) if reverse else (skip, CHUNK)
            pad_rows = lambda m: jnp.concatenate(
                ([jnp.zeros((lo, LANES), F32)] if lo else []) + [m]
                + ([jnp.zeros((CHUNK - hi, LANES), F32)] if hi < CHUNK else []), axis=0)
            add_rows = lambda x, dx: jnp.concatenate(
                ([x[:lo]] if lo else []) + [x[lo:hi] + dx] + ([x[hi:]] if hi < CHUNK else []),
                axis=0)
            squaring = it < n_steps - 1
            own = own2 if squaring else head_lanes
            for n in range(len(tasks)):
                w = jnp.concatenate([a_both[n], xs[n]], axis=1) if squaring else xs[n]
                z = jnp.concatenate([jnp.where(own[e], w, 0.0) for e in range(2)],
                                    axis=0).astype(BF16)
                rz = _dot(a_both[n][lo:hi].astype(BF16), z)
                if squaring:
                    a_both[n] = pad_rows(rz[:, :LANES])
                    xs[n] = add_rows(xs[n], rz[:, LANES:])
                else:
                    xs[n] = add_rows(xs[n], rz)

        ys = [[None] * N_PAIRS for _ in seqs]
        for t, (n, i, g) in enumerate(tasks):
            uv = jnp.concatenate([xs[t].astype(BF16), v_g[t]], axis=0)
            rbk = _dot(jnp.concatenate(bot[t], axis=0).astype(BF16), uv)
            ys[n][g] = from_state[t][CHUNK:] + jnp.where(head_lanes[0], rbk[:CHUNK], rbk[CHUNK:])
            upd = _dot_tn(uv, bk_o[t])
            s_ref[i, g] = state[t] * p_tot[t] + jnp.where(bd_mask, upd, 0.0)
        y = jnp.concatenate([jnp.concatenate(yn, axis=1) for yn in ys], axis=0)
        if reverse:
            out = jnp.concatenate([yf_ref[i] for i in seqs], axis=0) + y
            inv_n = 1.0 / HEAD_DIM
            mean = _seg_sum(out, seg) * inv_n
            cen = out - mean
            var = _seg_sum(cen * cen, seg) * inv_n
            out = cen * lax.rsqrt(var + GN_EPS) * gng_ref[...] + gnb_ref[...]
            y = (out + prep["bonus_v"]) * prep["gate"]
        for n, i in enumerate(seqs):
            out_ref[i] = y[n * CHUNK:(n + 1) * CHUNK].astype(out_ref.dtype)

    seqs = list(range(nb))
    scan(seqs, prepare(seqs))


def _wkv_call(reverse, p_rw, shared, dir_params, extra):
    b, t, _ = p_rw.shape
    n_chunks = t // CHUNK
    nb = math.gcd(b, WKV_SEQS)
    cur = lambda bi, c: (bi, (n_chunks - 1 - c) if reverse else c, 0)
    args = [p_rw] + list(dir_params) + list(shared)
    in_specs = [pl.BlockSpec((nb, CHUNK, RWKV_OUT_COLS), cur)] + [_full(a.shape) for a in args[1:]]
    if reverse:
        y_f = extra[0]
        args += [y_f] + list(extra[1:])
        in_specs += [pl.BlockSpec((nb, CHUNK, D_RWKV), cur)] + [_full(a.shape) for a in extra[1:]]
        out_dtype = BF16
    else:
        out_dtype = F32
    return pl.pallas_call(
        functools.partial(_wkv_kernel, reverse, n_chunks, nb),
        out_shape=jax.ShapeDtypeStruct((b, t, D_RWKV), out_dtype),
        grid=(b // nb, n_chunks),
        in_specs=in_specs,
        out_specs=pl.BlockSpec((nb, CHUNK, D_RWKV), cur),
        scratch_shapes=[pltpu.VMEM((nb, N_PAIRS, LANES, LANES), F32)],
        compiler_params=_params(("parallel", "arbitrary")),
        name="wkv_bwd" if reverse else "wkv_fwd",
    )(*args)


def _nat_kernel(n_rows, rb, q_ref, k_ref, v_ref, bias_ref, out_ref):
    i0 = pl.program_id(1) * rb
    lane = lax.broadcasted_iota(jnp.int32, (1, LANES), 1)
    head_lanes = (lane < HEAD_DIM, lane >= HEAD_DIM)
    starts, deltas = [], []
    for j in range(rb):
        rs = jnp.clip(i0 + j - WIN_H // 2, 0, n_rows - WIN_H)
        starts.append(pl.multiple_of(rs * GRID_W, GRID_W))
        deltas.append(i0 + j - rs)
    tasks = [(j, g) for j in range(rb) for g in range(N_PAIRS)]
    lanes = lambda g: slice(g * LANES, (g + 1) * LANES)
    kw = [k_ref[0, pl.ds(starts[j], WIN_TOKENS), lanes(g)] for j, g in tasks]
    vw = [v_ref[0, pl.ds(starts[j], WIN_TOKENS), lanes(g)] for j, g in tasks]
    qg = [q_ref[0, j * GRID_W:(j + 1) * GRID_W, lanes(g)].astype(F32) for j, g in tasks]
    s = [_dot_nt(jnp.concatenate([jnp.where(head_lanes[e], qg[n], 0.0) for e in range(2)],
                                 axis=0).astype(BF16), kw[n])
         + jnp.concatenate([bias_ref[2 * g + e, deltas[j]] for e in range(2)], axis=0)
         for n, (j, g) in enumerate(tasks)]
    pr = [jnp.exp2(sn - jnp.max(sn, axis=-1, keepdims=True)) for sn in s]
    o = [_dot(pr[n].astype(BF16), vw[n]) * (1.0 / jnp.sum(pr[n], axis=-1, keepdims=True))
         for n in range(len(tasks))]
    for n, (j, g) in enumerate(tasks):
        out_ref[0, j * GRID_W:(j + 1) * GRID_W, lanes(g)] = jnp.where(
            head_lanes[0], o[n][:GRID_W], o[n][GRID_W:]).astype(out_ref.dtype)


def _nat_bias_table(rpb):
    c = np.arange(GRID_W)[:, None]
    cp = np.arange(GRID_W)[None, :]
    cs = np.clip(c - WIN_W // 2, 0, GRID_W - WIN_W)
    valid = (cp >= cs) & (cp < cs + WIN_W)
    n_dj = 2 * WIN_W - 1
    n_di = 2 * WIN_H - 1
    onehot = ((cp - c + (WIN_W - 1))[None] == np.arange(n_dj)[:, None, None]) & valid[None]
    toe = jnp.einsum("hdj,jcx->hcdx", rpb.astype(F32), jnp.asarray(onehot, F32),
                     precision=lax.Precision.HIGHEST)
    toe = toe * LOG2_E + jnp.asarray(np.where(valid, 0.0, NEG_BIG), F32)[None, :, None, :]
    toe = toe.reshape(N_NAT_HEADS, GRID_W, n_di * GRID_W)
    tab = jnp.stack([toe[:, :, (WIN_H - 1 - dl) * GRID_W:(WIN_H - 1 - dl) * GRID_W + WIN_TOKENS]
                     for dl in range(WIN_H)], axis=1)
    return tab


def _nat_call(p_nat, bias_tab):
    b, t, _ = p_nat.shape
    n_rows = t // GRID_W
    rb = math.gcd(n_rows, NAT_ROWS)
    return pl.pallas_call(
        functools.partial(_nat_kernel, n_rows, rb),
        out_shape=jax.ShapeDtypeStruct((b, t, D_NAT), BF16),
        grid=(b, n_rows // rb),
        in_specs=[pl.BlockSpec((1, rb * GRID_W, D_NAT), lambda bi, i: (bi, i, 0)),
                  pl.BlockSpec((1, t, D_NAT), lambda bi, i: (bi, 0, 1)),
                  pl.BlockSpec((1, t, D_NAT), lambda bi, i: (bi, 0, 2)),
                  _full(bias_tab.shape)],
        out_specs=pl.BlockSpec((1, rb * GRID_W, D_NAT), lambda bi, i: (bi, i, 0)),
        compiler_params=_params(("parallel", "arbitrary")),
        name="nat",
    )(p_nat, p_nat, p_nat, bias_tab)


def _merge_xattn_kernel(batch_starts, *refs):
    n_src = len(batch_starts)
    x_refs = refs[:n_src]
    (yr_ref, yn_ref, gate_ref, kv_ref, wbr_ref, wbn_ref, wout_ref,
     gx_ref, wq_ref, wo_ref, out_ref) = refs[n_src:]
    bi = pl.program_id(0)
    x_in = x_refs[0][0]
    for s in range(1, n_src):
        x_in = jnp.where(bi >= batch_starts[s], x_refs[s][0], x_in)
    gates = gate_ref[0].astype(F32)
    mixed = (_sigmoid(gates[:, :D_MODEL]) * _dot(yr_ref[0], wbr_ref[...])
             + _sigmoid(gates[:, D_MODEL:]) * _dot(yn_ref[0], wbn_ref[...]))
    x1 = x_in + _dot(mixed.astype(BF16), wout_ref[...])
    q = _dot(_rmsnorm(x1, gx_ref[...]).astype(BF16), wq_ref[...])
    scale = XATTN_HEAD_DIM ** -0.5
    heads = []
    for h in range(N_XATTN_HEADS):
        sl = slice(h * XATTN_HEAD_DIM, (h + 1) * XATTN_HEAD_DIM)
        qh = (q[:, sl] * scale).astype(BF16)
        kh = kv_ref[0, :, sl]
        vh = kv_ref[0, :, D_MODEL + h * XATTN_HEAD_DIM:D_MODEL + (h + 1) * XATTN_HEAD_DIM]
        s = _dot_nt(qh, kh)
        m = jnp.max(s, axis=-1, keepdims=True)
        pr = jnp.exp(s - m)
        inv_l = 1.0 / jnp.sum(pr, axis=-1, keepdims=True)
        heads.append(_dot(pr.astype(BF16), vh) * inv_l)
    o = jnp.concatenate(heads, axis=1).astype(BF16)
    out_ref[0] = x1 + _dot(o, wo_ref[...])


def _merge_xattn_call(xs, y_r, y_n, gates, kv, wbr, wbn, wout, gx, wq, wo, tm):
    b, t, _ = y_r.shape
    d = xs[0].shape[2]
    n_mem = kv.shape[1]
    tile = lambda w: pl.BlockSpec((1, tm, w), lambda bi, i: (bi, i, 0))
    ws = [wbr, wbn, wout, gx.reshape(1, d), wq, wo]
    starts = tuple(int(v) for v in np.cumsum([0] + [x.shape[0] for x in xs[:-1]]))
    x_specs = [pl.BlockSpec((1, tm, d), lambda bi, i, st=st, nb=x.shape[0]:
                            (jnp.clip(bi - st, 0, nb - 1), i, 0)) for x, st in zip(xs, starts)]
    return pl.pallas_call(
        functools.partial(_merge_xattn_kernel, starts),
        out_shape=jax.ShapeDtypeStruct((b, t, d), F32),
        grid=(b, t // tm),
        in_specs=x_specs + [tile(D_RWKV), tile(D_NAT), tile(GATE_COLS),
                            pl.BlockSpec((1, n_mem, 2 * d), lambda bi, i: (bi, 0, 0))]
        + [_full(w.shape) for w in ws],
        out_specs=tile(d),
        compiler_params=_params(("parallel", "arbitrary")),
        name="merge_xattn",
    )(*xs, y_r, y_n, gates, kv, *ws)


def _ffn_kernel(final, ff_chunk, tile_starts, x_ref, g_ref, w1_ref, w2_ref, gf_ref, *out_refs):
    x = x_ref[...]
    h = _rmsnorm(x, g_ref[...]).astype(BF16)
    acc = x
    for j in range(D_FF // ff_chunk):
        sl = slice(j * ff_chunk, (j + 1) * ff_chunk)
        hf = jnp.maximum(_dot(h, w1_ref[:, sl]), 0.0)
        acc = acc + _dot((hf * hf).astype(BF16), w2_ref[sl, :])
    if final:
        acc = _rmsnorm(acc, gf_ref[...])
    if len(out_refs) == 1:
        out_refs[0][...] = acc
        return
    i = pl.program_id(0)
    bounds = list(tile_starts) + [None]
    for k, out_ref in enumerate(out_refs):
        lo, hi = bounds[k], bounds[k + 1]
        cond = (i >= lo) if hi is None else ((i >= lo) & (i < hi))

        @pl.when(cond)
        def _(out_ref=out_ref):
            out_ref[...] = acc


def _ffn_call(x2d, g, w1, w2, g_final, final, tm, out_rows):
    n, d = x2d.shape
    starts = _tile_starts(out_rows, tm)
    out_specs = [pl.BlockSpec((tm, d), lambda i, st=st, nt=r // tm: (jnp.clip(i - st, 0, nt - 1), 0))
                 for r, st in zip(out_rows, starts)]
    return pl.pallas_call(
        functools.partial(_ffn_kernel, final, 1024, starts),
        out_shape=[jax.ShapeDtypeStruct((r, d), F32) for r in out_rows],
        grid=(n // tm,),
        in_specs=[pl.BlockSpec((tm, d), lambda i: (i, 0)), _full((1, d)),
                  _full(w1.shape), _full(w2.shape), _full((1, d))],
        out_specs=out_specs,
        compiler_params=_params(("arbitrary",)),
        name="ffn",
    )(x2d, g.reshape(1, d), w1, w2, g_final.reshape(1, d))


def _pad_dir(w, d):
    z = jnp.zeros_like(w[d])
    parts = [w[0], z] if d == 0 else [z, w[1]]
    return jnp.concatenate(parts, axis=0).astype(BF16)


def _trunk(xs, mem, norm_mix, w_in, mu_prev, mu_next, w0, w_up, a0, a_up, g_up, k_k, k_a, r_k,
           gn_g, gn_b, rpb, w_br_rwkv, w_br_nat, w_out, norm_x, norm_mem, w_xq, w_xkv, w_xo,
           norm_ff, w_ff1, w_ff2, norm_final, tm=512):
    t, d = xs[0].shape[1:]
    assert d == D_MODEL and t % tm == 0 and t % (NAT_ROWS * GRID_W) == 0 and t >= WIN_TOKENS
    batches = [x.shape[0] for x in xs]
    b = sum(batches)
    n_mem = mem.shape[1]
    depth = w_in.shape[0]
    head_id = np.arange(2 * LANES) // HEAD_DIM
    seg = jnp.asarray(head_id[:, None] == head_id[None, :], dtype=BF16)
    row = lambda a: a.reshape(1, -1).astype(F32)
    mem2d = mem.reshape(b * n_mem, d)
    for l in range(depth):
        col_scale = np.ones((1, w_in.shape[2]), np.float32)
        col_scale[:, RWKV_COLS:RWKV_COLS + D_NAT] = HEAD_DIM ** -0.5 * LOG2_E
        w_in_l = (w_in[l] * col_scale).astype(BF16)
        p_rw, p_nat, p_gate = _in_proj(
            [x.reshape(-1, d) for x in xs], norm_mix[l], row(mu_prev[l]), row(mu_next[l]),
            row(k_k[l]), seg, w_in_l[:, :RWKV_COLS], w_in_l[:, RWKV_COLS:RWKV_COLS + NAT_COLS],
            w_in_l[:, RWKV_COLS + NAT_COLS:], t, tm)
        p_rw = p_rw.reshape(b, t, RWKV_OUT_COLS)
        p_nat = p_nat.reshape(b, t, NAT_COLS)
        p_gate = p_gate.reshape(b, t, GATE_COLS)

        shared = [row(k_a[l]), seg]
        dir_params = lambda dd: [row(0.5 * w0[l, dd]), _pad_dir(0.5 * w_up[l], dd),
                                 row(0.5 * a0[l, dd]), _pad_dir(0.5 * a_up[l], dd)]
        y_f = _wkv_call(False, p_rw, shared, dir_params(0), None)
        extra = [y_f, row(0.5 * a0[l, 0]), _pad_dir(0.5 * a_up[l], 0), g_up[l].astype(BF16), row(r_k[l]),
                 row(gn_g[l]), row(gn_b[l])]
        y_r = _wkv_call(True, p_rw, shared, dir_params(1), extra)

        y_n = _nat_call(p_nat, _nat_bias_table(rpb[l]))

        (kv,) = _norm_proj(mem2d, norm_mem[l], [w_xkv[l].astype(BF16)], [BF16], tm, "mem_kv")
        kv = kv.reshape(b, n_mem, 2 * d)
        x = _merge_xattn_call(xs, y_r, y_n, p_gate, kv, w_br_rwkv[l].astype(BF16),
                              w_br_nat[l].astype(BF16), w_out[l].astype(BF16), norm_x[l],
                              w_xq[l].astype(BF16), w_xo[l].astype(BF16), tm)
        last = l == depth - 1
        outs = _ffn_call(x.reshape(b * t, d), norm_ff[l], w_ff1[l].astype(BF16),
                         w_ff2[l].astype(BF16), norm_final, last, tm,
                         [bk * t for bk in batches] if last else [b * t])
        xs = [o.reshape(-1, t, d) for o in outs]
    return xs


def kernel(x_prompt, x_sample, mem_prompt, mem_sample, norm_mix, w_in, mu_prev, mu_next, w0, w_up, a0, a_up, g_up, k_k, k_a, r_k, gn_g, gn_b, rpb, w_br_rwkv, w_br_nat, w_out, norm_x, norm_mem, w_xq, w_xkv, w_xo, norm_ff, w_ff1, w_ff2, norm_final):
    assert x_prompt.shape[1:] == x_sample.shape[1:] and mem_prompt.shape[1:] == mem_sample.shape[1:]
    mem = jnp.concatenate([mem_prompt, mem_sample], axis=0)
    y_prompt, y_sample = _trunk(
        [x_prompt, x_sample], mem, norm_mix, w_in, mu_prev, mu_next, w0, w_up, a0, a_up, g_up, k_k,
        k_a, r_k, gn_g, gn_b, rpb, w_br_rwkv, w_br_nat, w_out, norm_x, norm_mem, w_xq, w_xkv, w_xo,
        norm_ff, w_ff1, w_ff2, norm_final)
    return (y_prompt, y_sample)
```

```python
import functools
import math

import numpy as np
import jax
import jax.numpy as jnp
from jax import lax
from jax.experimental import pallas as pl
from jax.experimental.pallas import tpu as pltpu

F32 = jnp.float32
BF16 = jnp.bfloat16

D_MODEL = 1024
GRID_W = 64
HEAD_DIM = 64
D_RWKV = 512
D_NAT = 512
N_NAT_HEADS = D_NAT // HEAD_DIM
LORA_W = 64
LORA_A = 64
LORA_G = 128
WIN_H = 8
WIN_W = 16
N_XATTN_HEADS = 4
XATTN_HEAD_DIM = D_MODEL // N_XATTN_HEADS
D_FF = 4 * D_MODEL
NORM_EPS = 1e-6
GN_EPS = 1e-5 * HEAD_DIM
RWKV_COLS = 3 * D_RWKV + 2 * LORA_W + 2 * LORA_A + LORA_G
NAT_COLS = 3 * D_NAT
GATE_COLS = 2 * D_MODEL
COL_WD = 3 * D_RWKV
COL_AD = COL_WD + 2 * LORA_W
COL_GD = COL_AD + 2 * LORA_A
COL_KKN = RWKV_COLS
RWKV_OUT_COLS = RWKV_COLS + D_RWKV

LANES = 128
N_PAIRS = D_RWKV // LANES
CHUNK = 64
ROW_TILE = 16
WKV_SEQS = 12
NAT_ROWS = 8
FFN_ROWS = 1024
WIN_TOKENS = WIN_H * GRID_W
NEG_BIG = -1e30
LOG2_E = math.log2(math.e)
VMEM_LIMIT = 56 * 1024 * 1024


def _dot(a, b):
    return jnp.dot(a, b, preferred_element_type=F32)


def _dot_nt(a, b):
    return lax.dot_general(a, b, (((1,), (1,)), ((), ())), preferred_element_type=F32)


def _dot_tn(a, b):
    return lax.dot_general(a, b, (((0,), (0,)), ((), ())), preferred_element_type=F32)


def _dot_split_lhs(a, b):
    hi = b.astype(BF16)
    lo = (b - hi.astype(F32)).astype(BF16)
    return _dot(a, hi) + _dot(a, lo)


def _sigmoid_of_half(xh):
    return 0.5 * jnp.tanh(xh) + 0.5


def _sigmoid(x):
    return _sigmoid_of_half(0.5 * x)


def _rmsnorm(x, g):
    return x * lax.rsqrt(jnp.mean(x * x, axis=-1, keepdims=True) + NORM_EPS) * g


def _params(sem):
    return pltpu.CompilerParams(dimension_semantics=sem, vmem_limit_bytes=VMEM_LIMIT)


def _full(shape):
    nd = len(shape)
    return pl.BlockSpec(shape, lambda *_: (0,) * nd, pipeline_mode=pl.Buffered(1))


def _norm_proj_kernel(n_out, x_ref, g_ref, *refs):
    w_refs, o_refs = refs[:n_out], refs[n_out:]
    h = _rmsnorm(x_ref[...], g_ref[...]).astype(BF16)
    for w_ref, o_ref in zip(w_refs, o_refs):
        o_ref[...] = _dot(h, w_ref[...]).astype(o_ref.dtype)


def _norm_proj(x2d, g, ws, out_dtypes, tm, name):
    n, d = x2d.shape
    tm = math.gcd(n, tm)
    n_out = len(ws)
    return pl.pallas_call(
        functools.partial(_norm_proj_kernel, n_out),
        out_shape=[jax.ShapeDtypeStruct((n, w.shape[1]), dt) for w, dt in zip(ws, out_dtypes)],
        grid=(n // tm,),
        in_specs=[pl.BlockSpec((tm, d), lambda i: (i, 0)), _full((1, d))]
        + [_full(w.shape) for w in ws],
        out_specs=[pl.BlockSpec((tm, w.shape[1]), lambda i: (i, 0)) for w in ws],
        compiler_params=_params(("parallel",)),
        name=name,
    )(x2d, g.reshape(1, d), *ws)


def _seg_sum(x, seg):
    w = seg.shape[0]
    return jnp.concatenate([_dot(x[:, j:j + w].astype(BF16), seg) for j in range(0, x.shape[1], w)],
                           axis=1)


def _tile_starts(row_counts, tm):
    return tuple(int(s) // tm for s in np.cumsum([0] + list(row_counts[:-1])))


def _in_proj_kernel(tiles_per_seq, tile_starts, *refs):
    n_src = len(tile_starts)
    src_refs = refs[:3 * n_src]
    (g_ref, mup_ref, mun_ref, kk_ref, seg_ref, wrw_ref, wnat_ref, wgate_ref,
     prw_ref, pnat_ref, pgate_ref) = refs[3 * n_src:]
    i = pl.program_id(0)

    def pick(k):
        val = src_refs[k][...]
        for s in range(1, n_src):
            val = jnp.where(i >= tile_starts[s], src_refs[3 * s + k][...], val)
        return val

    x = pick(0)
    tm = x.shape[0]
    g = g_ref[...]
    h = _rmsnorm(x, g)
    pos = i % tiles_per_seq
    keep_prev = jnp.where(pos == 0, 0.0, 1.0)
    keep_next = jnp.where(pos == tiles_per_seq - 1, 0.0, 1.0)
    h_ext = jnp.concatenate([_rmsnorm(pick(1), g) * keep_prev, h,
                             _rmsnorm(pick(2), g) * keep_next], axis=0)
    p = _dot(h_ext.astype(BF16), wrw_ref[...])
    p_mid = p[8:8 + tm]
    p_prev = pltpu.roll(p, 1, 0)[8:8 + tm]
    p_next = pltpu.roll(p, tm + 15, 0)[8:8 + tm]
    mup, mun = mup_ref[...], mun_ref[...]
    ps = (1.0 - mup - mun) * p_mid + mup * p_prev + mun * p_next
    prw_ref[:, :RWKV_COLS] = ps
    kkv = ps[:, D_RWKV:2 * D_RWKV] * kk_ref[...]
    prw_ref[:, COL_KKN:] = kkv * lax.rsqrt(jnp.maximum(_seg_sum(kkv * kkv, seg_ref[...]), 1e-24))
    hb = h.astype(BF16)
    pnat_ref[...] = _dot(hb, wnat_ref[...]).astype(pnat_ref.dtype)
    pgate_ref[...] = _dot(hb, wgate_ref[...]).astype(pgate_ref.dtype)


def _in_proj(xs, g, mup, mun, kk, seg, w_rw, w_nat, w_gate, seq_len, tm):
    d = xs[0].shape[1]
    n = sum(x.shape[0] for x in xs)
    per8 = tm // 8
    ws = [w_rw, w_nat, w_gate]
    starts = _tile_starts([x.shape[0] for x in xs], tm)
    src_specs, src_args = [], []
    for x, st in zip(xs, starts):
        nt = x.shape[0] // tm
        local = lambda i, st=st, nt=nt: jnp.clip(i - st, 0, nt - 1)
        last8 = x.shape[0] // 8 - 1
        src_specs += [
            pl.BlockSpec((tm, d), lambda i, local=local: (local(i), 0)),
            pl.BlockSpec((8, d), lambda i, local=local: (jnp.maximum(local(i) * per8 - 1, 0), 0)),
            pl.BlockSpec((8, d), lambda i, local=local, last8=last8:
                         (jnp.minimum((local(i) + 1) * per8, last8), 0))]
        src_args += [x, x, x]
    return pl.pallas_call(
        functools.partial(_in_proj_kernel, seq_len // tm, starts),
        out_shape=[jax.ShapeDtypeStruct((n, RWKV_OUT_COLS), F32),
                   jax.ShapeDtypeStruct((n, NAT_COLS), BF16),
                   jax.ShapeDtypeStruct((n, GATE_COLS), BF16)],
        grid=(n // tm,),
        in_specs=src_specs + [_full((1, d)), _full(mup.shape), _full(mun.shape), _full(kk.shape),
                              _full(seg.shape)] + [_full(w.shape) for w in ws],
        out_specs=[pl.BlockSpec((tm, c), lambda i: (i, 0))
                   for c in (RWKV_OUT_COLS, NAT_COLS, GATE_COLS)],
        compiler_params=_params(("parallel",)),
        name="in_proj",
    )(*src_args, g.reshape(1, d), mup, mun, kk, seg, *ws)


def _wkv_kernel(reverse, nb, *refs):
    if reverse:
        (p_ref, w0_ref, wup_ref, a0_ref, aup_ref,
         ka_ref, seg_ref, yf_ref, a0o_ref, aupo_ref, gup_ref, rk_ref, gng_ref, gnb_ref,
         out_ref, s_ref) = refs
    else:
        (p_ref, w0_ref, wup_ref, a0_ref, aup_ref,
         ka_ref, seg_ref, out_ref, s_ref) = refs

    c = pl.program_id(1)

    @pl.when(c == 0)
    def _():
        s_ref[...] = jnp.zeros_like(s_ref)

    seg = seg_ref[...]
    ti = lax.broadcasted_iota(jnp.int32, (CHUNK, CHUNK), 0)
    si = lax.broadcasted_iota(jnp.int32, (CHUNK, CHUNK), 1)
    tri = ((si >= ti) if reverse else (si <= ti)).astype(BF16)
    last = 0 if reverse else CHUNK - 1
    ri = lax.broadcasted_iota(jnp.int32, (2 * CHUNK, LANES), 0)
    li = lax.broadcasted_iota(jnp.int32, (2 * CHUNK, LANES), 1)
    ri2 = lax.broadcasted_iota(jnp.int32, (4 * CHUNK, LANES), 0)
    li2 = lax.broadcasted_iota(jnp.int32, (4 * CHUNK, LANES), 1)
    t_idx = ri2 % CHUNK
    s_idx = li2 % CHUNK
    incl = (ri2 // CHUNK) % 2
    sc_mask2 = (s_idx > t_idx - incl) if reverse else (s_idx < t_idx + incl)
    lane = lax.broadcasted_iota(jnp.int32, (1, LANES), 1)
    head_lanes = (lane < HEAD_DIM, lane >= HEAD_DIM)
    bd_mask = (ri // HEAD_DIM) == (li // HEAD_DIM)
    zeros_cv = jnp.zeros((CHUNK, LANES), BF16)
    n_steps = int(math.log2(CHUNK))

    def prepare(seqs):
        ps = jnp.concatenate([p_ref[i] for i in seqs], axis=0)
        r = ps[:, 0:D_RWKV]
        k = ps[:, D_RWKV:2 * D_RWKV]
        v = ps[:, 2 * D_RWKV:3 * D_RWKV]
        wd = ps[:, COL_WD:COL_AD]
        ad = ps[:, COL_AD:COL_GD]
        half_w = w0_ref[...] + _dot(jnp.tanh(wd).astype(BF16), wup_ref[...])
        half_c = -0.5 * math.exp(-0.5) * LOG2_E
        logw = half_c * jnp.tanh(half_w) + half_c
        ad_b = ad.astype(BF16)
        neg_a = -0.5 * jnp.tanh(a0_ref[...] + _dot(ad_b, aup_ref[...])) - 0.5
        kkn = ps[:, COL_KKN:RWKV_OUT_COLS]
        ka = ka_ref[...]
        one_m_ka, neg_ka = 1.0 - ka, -ka
        k_d = k * (one_m_ka + neg_a * neg_ka)
        b_vec = kkn * neg_a
        cums = [_dot_split_lhs(tri, logw[n * CHUNK:(n + 1) * CHUNK]) for n in range(len(seqs))]
        p_tots = [jnp.exp2(cm[last:last + 1]) for cm in cums]
        cum = jnp.concatenate(cums, axis=0) if len(seqs) > 1 else cums[0]
        a_t = kkn * jnp.exp2(cum - logw)
        r_t = r * jnp.exp2(cum)
        e_inv = jnp.exp2(-cum)
        prep = dict(a_t=a_t, r_t=r_t, b_t=b_vec * e_inv, k_t=k_d * e_inv, v=v, p_tots=p_tots)
        if reverse:
            neg_ao = -0.5 * jnp.tanh(a0o_ref[...] + _dot(ad_b, aupo_ref[...])) - 0.5
            k_sum = k_d + k * (one_m_ka + neg_ao * neg_ka)
            prep["bonus_v"] = _seg_sum(r * k_sum * rk_ref[...], seg) * v
            prep["gate"] = _dot(_sigmoid(ps[:, COL_GD:RWKV_COLS]).astype(BF16), gup_ref[...])
        return prep

    def scan(seqs, prep):
        tasks = [(n, i, g) for n, i in enumerate(seqs) for g in range(N_PAIRS)]

        def tile(a, n, g):
            return a[n * CHUNK:(n + 1) * CHUNK, g * LANES:(g + 1) * LANES]

        lhs_f = [jnp.concatenate([tile(prep["a_t"], n, g), tile(prep["r_t"], n, g)], axis=0)
                 for n, _, g in tasks]
        rhs_f = [jnp.concatenate([tile(prep["b_t"], n, g), tile(prep["k_t"], n, g)], axis=0)
                 for n, _, g in tasks]
        rhs = [rf.astype(BF16) for rf in rhs_f]
        p_tot = [prep["p_tots"][n][:, g * LANES:(g + 1) * LANES] for n, _, g in tasks]
        bk_o = [(rf * pt).astype(BF16) for rf, pt in zip(rhs_f, p_tot)]
        v_g = [tile(prep["v"], n, g).astype(BF16) for n, _, g in tasks]
        state = [s_ref[i, g] for _, i, g in tasks]
        from_state = [_dot_nt(lf.astype(BF16), st.astype(BF16)) for lf, st in zip(lhs_f, state)]
        sc2 = [jnp.where(sc_mask2, _dot_nt(jnp.concatenate(
            [jnp.where(head_lanes[e], lf, 0.0) for e in range(2)], axis=0).astype(BF16), rh), 0.0)
               for lf, rh in zip(lhs_f, rhs)]
        top = [[s2[2 * CHUNK * e:2 * CHUNK * e + CHUNK] for e in range(2)] for s2 in sc2]
        bot = [[s2[2 * CHUNK * e + CHUNK:2 * CHUNK * (e + 1)] for e in range(2)] for s2 in sc2]
        zv = [jnp.concatenate([zeros_cv, vg], axis=0) for vg in v_g]
        akv = [_dot(jnp.concatenate(tp, axis=0).astype(BF16), z) for tp, z in zip(top, zv)]
        x0 = [fs[:CHUNK] + jnp.where(head_lanes[0], ak[:CHUNK], ak[CHUNK:])
              for fs, ak in zip(from_state, akv)]

        a_both = [jnp.where(head_lanes[0], tp[0], pltpu.roll(tp[1], HEAD_DIM, 1)) for tp in top]
        xs = list(x0)
        lane2 = lax.broadcasted_iota(jnp.int32, (1, 2 * LANES), 1) % LANES
        own2 = (lane2 < HEAD_DIM, lane2 >= HEAD_DIM)
        for it in range(n_steps):
            skip = ((1 << it) // ROW_TILE) * ROW_TILE
            lo, hi = (0, CHUNK - skip) if reverse else (skip, CHUNK)
            pad_rows = lambda m: jnp.concatenate(
                ([jnp.zeros((lo, LANES), F32)] if lo else []) + [m]
                + ([jnp.zeros((CHUNK - hi, LANES), F32)] if hi < CHUNK else []), axis=0)
            add_rows = lambda x, dx: jnp.concatenate(
                ([x[:lo]] if lo else []) + [x[lo:hi] + dx] + ([x[hi:]] if hi < CHUNK else []),
                axis=0)
            squaring = it < n_steps - 1
            own = own2 if squaring else head_lanes
            for n in range(len(tasks)):
                w = jnp.concatenate([a_both[n], xs[n]], axis=1) if squaring else xs[n]
                z = jnp.concatenate([jnp.where(own[e], w, 0.0) for e in range(2)],
                                    axis=0).astype(BF16)
                rz = _dot(a_both[n][lo:hi].astype(BF16), z)
                if squaring:
                    a_both[n] = pad_rows(rz[:, :LANES])
                    xs[n] = add_rows(xs[n], rz[:, LANES:])
                else:
                    xs[n] = add_rows(xs[n], rz)

        ys = [[None] * N_PAIRS for _ in seqs]
        for t, (n, i, g) in enumerate(tasks):
            uv = jnp.concatenate([xs[t].astype(BF16), v_g[t]], axis=0)
            rbk = _dot(jnp.concatenate(bot[t], axis=0).astype(BF16), uv)
            ys[n][g] = from_state[t][CHUNK:] + jnp.where(head_lanes[0], rbk[:CHUNK], rbk[CHUNK:])
            upd = _dot_tn(uv, bk_o[t])
            s_ref[i, g] = state[t] * p_tot[t] + jnp.where(bd_mask, upd, 0.0)
        y = jnp.concatenate([jnp.concatenate(yn, axis=1) for yn in ys], axis=0)
        if reverse:
            out = jnp.concatenate([yf_ref[i] for i in seqs], axis=0) + y
            inv_n = 1.0 / HEAD_DIM
            mean = _seg_sum(out, seg) * inv_n
            cen = out - mean
            var = _seg_sum(cen * cen, seg) * inv_n
            out = cen * lax.rsqrt(var + GN_EPS) * gng_ref[...] + gnb_ref[...]
            y = (out + prep["bonus_v"]) * prep["gate"]
        for n, i in enumerate(seqs):
            out_ref[i] = y[n * CHUNK:(n + 1) * CHUNK].astype(out_ref.dtype)

    seqs = list(range(nb))
    scan(seqs, prepare(seqs))


def _wkv_call(reverse, p_rw, shared, dir_params, extra):
    b, t, _ = p_rw.shape
    n_chunks = t // CHUNK
    nb = math.gcd(b, WKV_SEQS)
    cur = lambda bi, c: (bi, (n_chunks - 1 - c) if reverse else c, 0)
    args = [p_rw] + list(dir_params) + list(shared)
    in_specs = [pl.BlockSpec((nb, CHUNK, RWKV_OUT_COLS), cur)] + [_full(a.shape) for a in args[1:]]
    if reverse:
        y_f = extra[0]
        args += [y_f] + list(extra[1:])
        in_specs += [pl.BlockSpec((nb, CHUNK, D_RWKV), cur)] + [_full(a.shape) for a in extra[1:]]
        out_dtype = BF16
    else:
        out_dtype = F32
    return pl.pallas_call(
        functools.partial(_wkv_kernel, reverse, nb),
        out_shape=jax.ShapeDtypeStruct((b, t, D_RWKV), out_dtype),
        grid=(b // nb, n_chunks),
        in_specs=in_specs,
        out_specs=pl.BlockSpec((nb, CHUNK, D_RWKV), cur),
        scratch_shapes=[pltpu.VMEM((nb, N_PAIRS, LANES, LANES), F32)],
        compiler_params=_params(("parallel", "arbitrary")),
        name="wkv_bwd" if reverse else "wkv_fwd",
    )(*args)


def _nat_kernel(n_rows, rb, q_ref, k_ref, v_ref, bias_ref, out_ref):
    i0 = pl.program_id(1) * rb
    lane = lax.broadcasted_iota(jnp.int32, (1, LANES), 1)
    head_lanes = (lane < HEAD_DIM, lane >= HEAD_DIM)
    starts, deltas = [], []
    for j in range(rb):
        rs = jnp.clip(i0 + j - WIN_H // 2, 0, n_rows - WIN_H)
        starts.append(pl.multiple_of(rs * GRID_W, GRID_W))
        deltas.append(i0 + j - rs)
    tasks = [(j, g) for j in range(rb) for g in range(N_PAIRS)]
    lanes = lambda g: slice(g * LANES, (g + 1) * LANES)
    kw = [k_ref[0, pl.ds(starts[j], WIN_TOKENS), lanes(g)] for j, g in tasks]
    vw = [v_ref[0, pl.ds(starts[j], WIN_TOKENS), lanes(g)] for j, g in tasks]
    qg = [q_ref[0, j * GRID_W:(j + 1) * GRID_W, lanes(g)].astype(F32) for j, g in tasks]
    s = [_dot_nt(jnp.concatenate([jnp.where(head_lanes[e], qg[n], 0.0) for e in range(2)],
                                 axis=0).astype(BF16), kw[n])
         + jnp.concatenate([bias_ref[2 * g + e, deltas[j]] for e in range(2)], axis=0)
         for n, (j, g) in enumerate(tasks)]
    pr = [jnp.exp2(sn - jnp.max(sn, axis=-1, keepdims=True)) for sn in s]
    o = [_dot(pr[n].astype(BF16), vw[n]) * (1.0 / jnp.sum(pr[n], axis=-1, keepdims=True))
         for n in range(len(tasks))]
    for n, (j, g) in enumerate(tasks):
        out_ref[0, j * GRID_W:(j + 1) * GRID_W, lanes(g)] = jnp.where(
            head_lanes[0], o[n][:GRID_W], o[n][GRID_W:]).astype(out_ref.dtype)


def _nat_bias_table(rpb):
    c = np.arange(GRID_W)[:, None]
    cp = np.arange(GRID_W)[None, :]
    cs = np.clip(c - WIN_W // 2, 0, GRID_W - WIN_W)
    valid = (cp >= cs) & (cp < cs + WIN_W)
    n_dj = 2 * WIN_W - 1
    n_di = 2 * WIN_H - 1
    onehot = ((cp - c + (WIN_W - 1))[None] == np.arange(n_dj)[:, None, None]) & valid[None]
    toe = jnp.einsum("hdj,jcx->hcdx", rpb.astype(F32), jnp.asarray(onehot, F32),
                     precision=lax.Precision.HIGHEST)
    toe = toe * LOG2_E + jnp.asarray(np.where(valid, 0.0, NEG_BIG), F32)[None, :, None, :]
    toe = toe.reshape(N_NAT_HEADS, GRID_W, n_di * GRID_W)
    tab = jnp.stack([toe[:, :, (WIN_H - 1 - dl) * GRID_W:(WIN_H - 1 - dl) * GRID_W + WIN_TOKENS]
                     for dl in range(WIN_H)], axis=1)
    return tab


def _nat_call(p_nat, bias_tab):
    b, t, _ = p_nat.shape
    n_rows = t // GRID_W
    rb = math.gcd(n_rows, NAT_ROWS)
    return pl.pallas_call(
        functools.partial(_nat_kernel, n_rows, rb),
        out_shape=jax.ShapeDtypeStruct((b, t, D_NAT), BF16),
        grid=(b, n_rows // rb),
        in_specs=[pl.BlockSpec((1, rb * GRID_W, D_NAT), lambda bi, i: (bi, i, 0)),
                  pl.BlockSpec((1, t, D_NAT), lambda bi, i: (bi, 0, 1)),
                  pl.BlockSpec((1, t, D_NAT), lambda bi, i: (bi, 0, 2)),
                  _full(bias_tab.shape)],
        out_specs=pl.BlockSpec((1, rb * GRID_W, D_NAT), lambda bi, i: (bi, i, 0)),
        compiler_params=_params(("parallel", "arbitrary")),
        name="nat",
    )(p_nat, p_nat, p_nat, bias_tab)


def _merge_xattn_kernel(batch_starts, *refs):
    n_src = len(batch_starts)
    x_refs = refs[:n_src]
    (yr_ref, yn_ref, gate_ref, kv_ref, wbr_ref, wbn_ref, wout_ref,
     gx_ref, wq_ref, wo_ref, out_ref) = refs[n_src:]
    bi = pl.program_id(0)
    x_in = x_refs[0][0]
    for s in range(1, n_src):
        x_in = jnp.where(bi >= batch_starts[s], x_refs[s][0], x_in)
    gates = gate_ref[0].astype(F32)
    mixed = (_sigmoid(gates[:, :D_MODEL]) * _dot(yr_ref[0], wbr_ref[...])
             + _sigmoid(gates[:, D_MODEL:]) * _dot(yn_ref[0], wbn_ref[...]))
    x1 = x_in + _dot(mixed.astype(BF16), wout_ref[...])
    q = _dot(_rmsnorm(x1, gx_ref[...]).astype(BF16), wq_ref[...])
    heads = []
    for h in range(N_XATTN_HEADS):
        sl = slice(h * XATTN_HEAD_DIM, (h + 1) * XATTN_HEAD_DIM)
        qh = q[:, sl].astype(BF16)
        kh = kv_ref[0, :, sl]
        vh = kv_ref[0, :, D_MODEL + h * XATTN_HEAD_DIM:D_MODEL + (h + 1) * XATTN_HEAD_DIM]
        s = _dot_nt(qh, kh)
        m = jnp.max(s, axis=-1, keepdims=True)
        pr = jnp.exp2(s - m)
        inv_l = 1.0 / jnp.sum(pr, axis=-1, keepdims=True)
        heads.append(_dot(pr.astype(BF16), vh) * inv_l)
    o = jnp.concatenate(heads, axis=1).astype(BF16)
    out_ref[0] = x1 + _dot(o, wo_ref[...])


def _merge_xattn_call(xs, y_r, y_n, gates, kv, wbr, wbn, wout, gx, wq, wo, tm):
    b, t, _ = y_r.shape
    d = xs[0].shape[2]
    n_mem = kv.shape[1]
    tile = lambda w: pl.BlockSpec((1, tm, w), lambda bi, i: (bi, i, 0))
    ws = [wbr, wbn, wout, gx.reshape(1, d), wq, wo]
    starts = tuple(int(v) for v in np.cumsum([0] + [x.shape[0] for x in xs[:-1]]))
    x_specs = [pl.BlockSpec((1, tm, d), lambda bi, i, st=st, nb=x.shape[0]:
                            (jnp.clip(bi - st, 0, nb - 1), i, 0)) for x, st in zip(xs, starts)]
    return pl.pallas_call(
        functools.partial(_merge_xattn_kernel, starts),
        out_shape=jax.ShapeDtypeStruct((b, t, d), F32),
        grid=(b, t // tm),
        in_specs=x_specs + [tile(D_RWKV), tile(D_NAT), tile(GATE_COLS),
                            pl.BlockSpec((1, n_mem, 2 * d), lambda bi, i: (bi, 0, 0))]
        + [_full(w.shape) for w in ws],
        out_specs=tile(d),
        compiler_params=_params(("parallel", "arbitrary")),
        name="merge_xattn",
    )(*xs, y_r, y_n, gates, kv, *ws)


def _ffn_kernel(final, ff_chunk, tile_starts, x_ref, g_ref, w1_ref, w2_ref, gf_ref, *out_refs):
    x = x_ref[...]
    h = _rmsnorm(x, g_ref[...]).astype(BF16)
    acc = x
    for j in range(D_FF // ff_chunk):
        sl = slice(j * ff_chunk, (j + 1) * ff_chunk)
        hf = jnp.maximum(_dot(h, w1_ref[:, sl]), 0.0)
        acc = acc + _dot((hf * hf).astype(BF16), w2_ref[sl, :])
    if final:
        acc = _rmsnorm(acc, gf_ref[...])
    if len(out_refs) == 1:
        out_refs[0][...] = acc
        return
    i = pl.program_id(0)
    bounds = list(tile_starts) + [None]
    for k, out_ref in enumerate(out_refs):
        lo, hi = bounds[k], bounds[k + 1]
        cond = (i >= lo) if hi is None else ((i >= lo) & (i < hi))

        @pl.when(cond)
        def _(out_ref=out_ref):
            out_ref[...] = acc


def _ffn_call(x2d, g, w1, w2, g_final, final, tm, out_rows):
    n, d = x2d.shape
    tm = math.gcd(tm, *out_rows)
    starts = _tile_starts(out_rows, tm)
    out_specs = [pl.BlockSpec((tm, d), lambda i, st=st, nt=r // tm: (jnp.clip(i - st, 0, nt - 1), 0))
                 for r, st in zip(out_rows, starts)]
    return pl.pallas_call(
        functools.partial(_ffn_kernel, final, 1024, starts),
        out_shape=[jax.ShapeDtypeStruct((r, d), F32) for r in out_rows],
        grid=(n // tm,),
        in_specs=[pl.BlockSpec((tm, d), lambda i: (i, 0)), _full((1, d)),
                  _full(w1.shape), _full(w2.shape), _full((1, d))],
        out_specs=out_specs,
        compiler_params=_params(("arbitrary",)),
        name="ffn",
    )(x2d, g.reshape(1, d), w1, w2, g_final.reshape(1, d))


def _pad_dir(w, d):
    z = jnp.zeros_like(w[d])
    parts = [w[0], z] if d == 0 else [z, w[1]]
    return jnp.concatenate(parts, axis=0).astype(BF16)


def _trunk(xs, mem, norm_mix, w_in, mu_prev, mu_next, w0, w_up, a0, a_up, g_up, k_k, k_a, r_k,
           gn_g, gn_b, rpb, w_br_rwkv, w_br_nat, w_out, norm_x, norm_mem, w_xq, w_xkv, w_xo,
           norm_ff, w_ff1, w_ff2, norm_final, tm=512):
    t, d = xs[0].shape[1:]
    assert d == D_MODEL and t % tm == 0 and t % (NAT_ROWS * GRID_W) == 0 and t >= WIN_TOKENS
    batches = [x.shape[0] for x in xs]
    b = sum(batches)
    n_mem = mem.shape[1]
    depth = w_in.shape[0]
    head_id = np.arange(2 * LANES) // HEAD_DIM
    seg = jnp.asarray(head_id[:, None] == head_id[None, :], dtype=BF16)
    row = lambda a: a.reshape(1, -1).astype(F32)
    mem2d = mem.reshape(b * n_mem, d)
    for l in range(depth):
        col_scale = np.ones((1, w_in.shape[2]), np.float32)
        col_scale[:, RWKV_COLS:RWKV_COLS + D_NAT] = HEAD_DIM ** -0.5 * LOG2_E
        w_in_l = (w_in[l] * col_scale).astype(BF16)
        p_rw, p_nat, p_gate = _in_proj(
            [x.reshape(-1, d) for x in xs], norm_mix[l], row(mu_prev[l]), row(mu_next[l]),
            row(k_k[l]), seg, w_in_l[:, :RWKV_COLS], w_in_l[:, RWKV_COLS:RWKV_COLS + NAT_COLS],
            w_in_l[:, RWKV_COLS + NAT_COLS:], t, tm)
        p_rw = p_rw.reshape(b, t, RWKV_OUT_COLS)
        p_nat = p_nat.reshape(b, t, NAT_COLS)
        p_gate = p_gate.reshape(b, t, GATE_COLS)

        shared = [row(k_a[l]), seg]
        dir_params = lambda dd: [row(0.5 * w0[l, dd]), _pad_dir(0.5 * w_up[l], dd),
                                 row(0.5 * a0[l, dd]), _pad_dir(0.5 * a_up[l], dd)]
        y_f = _wkv_call(False, p_rw, shared, dir_params(0), None)
        extra = [y_f, row(0.5 * a0[l, 0]), _pad_dir(0.5 * a_up[l], 0), g_up[l].astype(BF16), row(r_k[l]),
                 row(gn_g[l]), row(gn_b[l])]
        y_r = _wkv_call(True, p_rw, shared, dir_params(1), extra)

        y_n = _nat_call(p_nat, _nat_bias_table(rpb[l]))

        (kv,) = _norm_proj(mem2d, norm_mem[l], [w_xkv[l].astype(BF16)], [BF16], tm, "mem_kv")
        kv = kv.reshape(b, n_mem, 2 * d)
        x = _merge_xattn_call(xs, y_r, y_n, p_gate, kv, w_br_rwkv[l].astype(BF16),
                              w_br_nat[l].astype(BF16), w_out[l].astype(BF16), norm_x[l],
                              (w_xq[l] * (XATTN_HEAD_DIM ** -0.5 * LOG2_E)).astype(BF16),
                              w_xo[l].astype(BF16), tm)
        last = l == depth - 1
        outs = _ffn_call(x.reshape(b * t, d), norm_ff[l], w_ff1[l].astype(BF16),
                         w_ff2[l].astype(BF16), norm_final, last, FFN_ROWS,
                         [bk * t for bk in batches] if last else [b * t])
        xs = [o.reshape(-1, t, d) for o in outs]
    return xs


def kernel(x_prompt, x_sample, mem_prompt, mem_sample, norm_mix, w_in, mu_prev, mu_next, w0, w_up, a0, a_up, g_up, k_k, k_a, r_k, gn_g, gn_b, rpb, w_br_rwkv, w_br_nat, w_out, norm_x, norm_mem, w_xq, w_xkv, w_xo, norm_ff, w_ff1, w_ff2, norm_final):
    assert x_prompt.shape[1:] == x_sample.shape[1:] and mem_prompt.shape[1:] == mem_sample.shape[1:]
    mem = jnp.concatenate([mem_prompt, mem_sample], axis=0)
    y_prompt, y_sample = _trunk(
        [x_prompt, x_sample], mem, norm_mix, w_in, mu_prev, mu_next, w0, w_up, a0, a_up, g_up, k_k,
        k_a, r_k, gn_g, gn_b, rpb, w_br_rwkv, w_br_nat, w_out, norm_x, norm_mem, w_xq, w_xkv, w_xo,
        norm_ff, w_ff1, w_ff2, norm_final)
    return (y_prompt, y_sample)
```
